```python
import jax
import jax.numpy as jnp
from jax import lax
import numpy as np


D_MODEL = 1024
BATCH = 4
SEQ = 8192
DEPTH = 1

N_META = 16
ATT_HEADS = 8
ATT_KV_HEADS = 2
HEAD_DIM = 64
ATT_GROUP = ATT_HEADS // ATT_KV_HEADS
WINDOW = 128
BLOCK = 128
ROPE_THETA = 500000.0
ROT_DIM = HEAD_DIM // 4
ATT_Q = ATT_HEADS * HEAD_DIM
ATT_KV = ATT_KV_HEADS * HEAD_DIM
MASK_VALUE = -1e30
RWKV_HEADS = 8
RWKV_HEAD = 64
RWKV_WIDTH = RWKV_HEADS * RWKV_HEAD
DECAY_LORA = 64
AAA_LORA = 64
GATE_LORA = 128
GN_EPS = 64e-5
RWKV_IN = 3 * RWKV_WIDTH + DECAY_LORA + AAA_LORA + GATE_LORA
IN_COLS = ATT_Q + 2 * ATT_KV + RWKV_IN + 2 * D_MODEL
D_FF = 2816
NORM_EPS = 1e-6

kernel_name = 'hybrid_swa_sink_rwkv7_macaron_meta'


def rms_norm(x, gain):
    xf = x.astype(jnp.float32)
    y = xf * lax.rsqrt(jnp.mean(xf * xf, axis=-1, keepdims=True) + NORM_EPS)
    return (y * gain.astype(jnp.float32)).astype(x.dtype)


def swiglu(x, w_gate_up, w_down):
    gate, up = jnp.split(x @ w_gate_up, 2, axis=-1)
    return (jax.nn.silu(gate) * up) @ w_down


def partial_rope(x, pos):
    half = ROT_DIM // 2
    inv_freq = 1.0 / (ROPE_THETA ** (jnp.arange(half, dtype=jnp.float32) * (2.0 / ROT_DIM)))
    ang = pos.astype(jnp.float32)[:, None] * inv_freq[None, :]
    cos = jnp.cos(ang)[None, :, None, :]
    sin = jnp.sin(ang)[None, :, None, :]
    xr = x[..., :ROT_DIM].astype(jnp.float32)
    x1, x2 = xr[..., :half], xr[..., half:]
    rot = jnp.concatenate([x1 * cos - x2 * sin, x2 * cos + x1 * sin], axis=-1).astype(x.dtype)
    return jnp.concatenate([rot, x[..., ROT_DIM:]], axis=-1)


def sliding_window_gqa_sinks(q, k, v, sinks):
    B, T = q.shape[0], q.shape[1]
    pad = (-T) % BLOCK
    Tp = T + pad
    nb = Tp // BLOCK
    padw = ((0, 0), (pad, 0), (0, 0), (0, 0))
    qb = jnp.pad(q, padw).reshape(B, nb, BLOCK, ATT_KV_HEADS, ATT_GROUP, HEAD_DIM)
    kb = jnp.pad(k, padw).reshape(B, nb, BLOCK, ATT_KV_HEADS, HEAD_DIM)
    vb = jnp.pad(v, padw).reshape(B, nb, BLOCK, ATT_KV_HEADS, HEAD_DIM)
    kw = jnp.concatenate([jnp.concatenate([jnp.zeros_like(kb[:, :1]), kb[:, :-1]], axis=1), kb], axis=2)
    vw = jnp.concatenate([jnp.concatenate([jnp.zeros_like(vb[:, :1]), vb[:, :-1]], axis=1), vb], axis=2)
    s = jnp.einsum('bnqhgd,bnkhd->bnhgqk', qb, kw, preferred_element_type=jnp.float32)
    s = s * (HEAD_DIM ** -0.5)
    blk = jnp.arange(nb)[:, None, None]
    q_idx = blk * BLOCK + jnp.arange(BLOCK)[None, :, None]
    k_idx = (blk - 1) * BLOCK + jnp.arange(2 * BLOCK)[None, None, :]
    rel = q_idx - k_idx
    mask = (rel >= 0) & (rel < WINDOW) & (k_idx >= pad)
    s = jnp.where(mask[None, :, None, None], s, MASK_VALUE)
    sink = jnp.broadcast_to(
        sinks.astype(jnp.float32).reshape(ATT_KV_HEADS, ATT_GROUP)[None, None, :, :, None, None],
        s.shape[:-1] + (1,))
    p = jax.nn.softmax(jnp.concatenate([s, sink], axis=-1), axis=-1)[..., :-1]
    o = jnp.einsum('bnhgqk,bnkhd->bnqhgd', p.astype(v.dtype), vw)
    return o.reshape(B, Tp, ATT_Q)[:, pad:]


def token_shift(z, mu):
    z_prev = jnp.pad(z, ((0, 0), (1, 0), (0, 0)))[:, :-1]
    return z + (z_prev - z) * mu


def wkv7_scan(r, w, k, v, a, b):
    B, T, H, N = r.shape

    def step(S, inp):
        r_t, w_t, k_t, v_t, a_t, b_t = inp
        sa = jnp.einsum('bhvk,bhk->bhv', S, a_t)
        S = S * w_t[:, :, None, :] + sa[..., None] * b_t[:, :, None, :] + v_t[..., None] * k_t[:, :, None, :]
        return S, jnp.einsum('bhvk,bhk->bhv', S, r_t)

    xs = (jnp.moveaxis(r, 1, 0), jnp.moveaxis(w, 1, 0), jnp.moveaxis(k, 1, 0),
          jnp.moveaxis(v, 1, 0), jnp.moveaxis(a, 1, 0), jnp.moveaxis(b, 1, 0))
    S0 = jnp.zeros((B, H, N, N), jnp.float32)
    _, ys = lax.scan(step, S0, xs)
    return jnp.moveaxis(ys, 0, 1)


def rwkv7_time_mix(z, mu, w0, w2, a0, a2, g2, k_k, k_a, r_k, ln_w, ln_b):
    B, T = z.shape[0], z.shape[1]
    f32 = jnp.float32
    z = token_shift(z, mu).astype(f32)
    o1 = RWKV_WIDTH
    o4 = 3 * RWKV_WIDTH + DECAY_LORA
    r, k, v, xw, xa, xg = jnp.split(z, [o1, 2 * o1, 3 * o1, o4, o4 + AAA_LORA], axis=-1)
    w = -jax.nn.softplus(-(w0.astype(f32) + jnp.tanh(xw) @ w2.astype(f32))) - 0.5
    a = jax.nn.sigmoid(a0.astype(f32) + xa @ a2.astype(f32))
    g = jax.nn.sigmoid(xg) @ g2.astype(f32)
    hs = (B, T, RWKV_HEADS, RWKV_HEAD)
    kk = (k * k_k.astype(f32)).reshape(hs)
    kk = kk / jnp.maximum(jnp.sqrt(jnp.sum(kk * kk, axis=-1, keepdims=True)), 1e-12)
    k = k * (1.0 + (a - 1.0) * k_a.astype(f32))
    decay = jnp.exp(-jnp.exp(w))
    r_h, k_h, v_h, a_h = r.reshape(hs), k.reshape(hs), v.reshape(hs), a.reshape(hs)
    y = wkv7_scan(r_h, decay.reshape(hs), k_h, v_h, -kk, kk * a_h)
    mean = jnp.mean(y, axis=-1, keepdims=True)
    var = jnp.mean(jnp.square(y - mean), axis=-1, keepdims=True)
    y = ((y - mean) * lax.rsqrt(var + GN_EPS)).reshape(B, T, RWKV_WIDTH)
    y = y * ln_w.astype(f32) + ln_b.astype(f32)
    bonus = jnp.sum(r_h * k_h * r_k.astype(f32), axis=-1, keepdims=True) * v_h
    return ((y + bonus.reshape(B, T, RWKV_WIDTH)) * g).astype(mu.dtype)


def setup_inputs(seed: int = 0) -> dict:
    key = jax.random.key(seed)
    ks = jax.random.split(key, 32)
    f32 = jnp.float32

    def nrm(k, shape, scale):
        return jax.random.normal(k, shape, f32) * scale

    def gain(k, n):
        return 1.0 + 0.1 * jax.random.normal(k, (DEPTH, n), f32)

    return {
        'x': nrm(ks[0], (BATCH, SEQ, D_MODEL), 1.0),
        'meta_tokens': nrm(ks[1], (N_META, D_MODEL), 1.0),
        'ffn1_norm_pre': gain(ks[2], D_MODEL),
        'ffn1_w_gate_up': nrm(ks[3], (DEPTH, D_MODEL, 2 * D_FF), D_MODEL ** -0.5),
        'ffn1_w_down': nrm(ks[4], (DEPTH, D_FF, D_MODEL), D_FF ** -0.5),
        'ffn1_norm_post': gain(ks[5], D_MODEL),
        'mix_norm_pre': gain(ks[6], D_MODEL),
        'w_in': nrm(ks[7], (DEPTH, D_MODEL, IN_COLS), D_MODEL ** -0.5),
        'att_sinks': nrm(ks[8], (DEPTH, ATT_HEADS), 0.5),
        'rwkv_mu': jax.random.uniform(ks[9], (DEPTH, RWKV_IN), f32, 0.0, 1.0),
        'rwkv_w0': jax.random.uniform(ks[10], (DEPTH, RWKV_WIDTH), f32, -4.0, 1.0),
        'rwkv_w2': nrm(ks[11], (DEPTH, DECAY_LORA, RWKV_WIDTH), DECAY_LORA ** -0.5),
        'rwkv_a0': nrm(ks[12], (DEPTH, RWKV_WIDTH), 0.1),
        'rwkv_a2': nrm(ks[13], (DEPTH, AAA_LORA, RWKV_WIDTH), AAA_LORA ** -0.5),
        'rwkv_g2': nrm(ks[14], (DEPTH, GATE_LORA, RWKV_WIDTH), GATE_LORA ** -0.5),
        'rwkv_k_k': 0.85 + nrm(ks[15], (DEPTH, RWKV_WIDTH), 0.05),
        'rwkv_k_a': 1.0 + nrm(ks[16], (DEPTH, RWKV_WIDTH), 0.05),
        'rwkv_r_k': nrm(ks[17], (DEPTH, RWKV_HEADS, RWKV_HEAD), 0.1),
        'rwkv_ln_w': gain(ks[18], RWKV_WIDTH),
        'rwkv_ln_b': nrm(ks[19], (DEPTH, RWKV_WIDTH), 0.02),
        'w_att_branch': nrm(ks[20], (DEPTH, ATT_Q, D_MODEL), ATT_Q ** -0.5),
        'w_rwkv_branch': nrm(ks[21], (DEPTH, RWKV_WIDTH, D_MODEL), RWKV_WIDTH ** -0.5),
        'w_mix_out': nrm(ks[22], (DEPTH, D_MODEL, D_MODEL), D_MODEL ** -0.5),
        'mix_norm_post': gain(ks[23], D_MODEL),
        'ffn2_norm_pre': gain(ks[24], D_MODEL),
        'ffn2_w_gate_up': nrm(ks[25], (DEPTH, D_MODEL, 2 * D_FF), D_MODEL ** -0.5),
        'ffn2_w_down': nrm(ks[26], (DEPTH, D_FF, D_MODEL), D_FF ** -0.5),
        'ffn2_norm_post': gain(ks[27], D_MODEL),
    }


def reference(x, meta_tokens, ffn1_norm_pre, ffn1_w_gate_up, ffn1_w_down, ffn1_norm_post,
              mix_norm_pre, w_in, att_sinks, rwkv_mu, rwkv_w0, rwkv_w2, rwkv_a0, rwkv_a2,
              rwkv_g2, rwkv_k_k, rwkv_k_a, rwkv_r_k, rwkv_ln_w, rwkv_ln_b, w_att_branch,
              w_rwkv_branch, w_mix_out, mix_norm_post, ffn2_norm_pre, ffn2_w_gate_up,
              ffn2_w_down, ffn2_norm_post):
    B = x.shape[0]
    meta = jnp.broadcast_to(meta_tokens.astype(x.dtype)[None], (B, N_META, D_MODEL))
    h = jnp.concatenate([meta, x], axis=1)
    T = h.shape[1]
    pos = jnp.arange(T, dtype=jnp.int32)
    c_q = ATT_Q
    c_k = c_q + ATT_KV
    c_v = c_k + ATT_KV
    c_r = c_v + RWKV_IN
    c_ga = c_r + D_MODEL
    for l in range(DEPTH):
        f = swiglu(rms_norm(h, ffn1_norm_pre[l]), ffn1_w_gate_up[l], ffn1_w_down[l])
        h = h + 0.5 * rms_norm(f, ffn1_norm_post[l])
        u = rms_norm(h, mix_norm_pre[l])
        z = u @ w_in[l]
        q, k, v, zr, ga, gr = jnp.split(z, [c_q, c_k, c_v, c_r, c_ga], axis=-1)
        q = partial_rope(q.reshape(B, T, ATT_HEADS, HEAD_DIM), pos)
        k = partial_rope(k.reshape(B, T, ATT_KV_HEADS, HEAD_DIM), pos)
        v = v.reshape(B, T, ATT_KV_HEADS, HEAD_DIM)
        y_att = sliding_window_gqa_sinks(q, k, v, att_sinks[l]) @ w_att_branch[l]
        y_rwkv = rwkv7_time_mix(zr, rwkv_mu[l], rwkv_w0[l], rwkv_w2[l], rwkv_a0[l], rwkv_a2[l],
                                rwkv_g2[l], rwkv_k_k[l], rwkv_k_a[l], rwkv_r_k[l],
                                rwkv_ln_w[l], rwkv_ln_b[l]) @ w_rwkv_branch[l]
        merged = jax.nn.sigmoid(ga) * y_att + jax.nn.sigmoid(gr) * y_rwkv
        h = h + rms_norm(merged @ w_mix_out[l], mix_norm_post[l])
        f = swiglu(rms_norm(h, ffn2_norm_pre[l]), ffn2_w_gate_up[l], ffn2_w_down[l])
        h = h + 0.5 * rms_norm(f, ffn2_norm_post[l])
    return h[:, N_META:]
```

```python
import functools

import jax
import jax.numpy as jnp
from jax import lax
from jax.experimental import pallas as pl
from jax.experimental.pallas import tpu as pltpu

D_MODEL = 1024
N_META = 16
ATT_HEADS = 8
ATT_KV_HEADS = 2
ATT_GROUP = ATT_HEADS // ATT_KV_HEADS
HEAD_DIM = 64
WINDOW = 128
BLOCK = 128
ROPE_THETA = 500000.0
ROT_DIM = HEAD_DIM // 4
ATT_Q = ATT_HEADS * HEAD_DIM
ATT_KV = ATT_KV_HEADS * HEAD_DIM
MASK_VALUE = -1e30
RWKV_HEADS = 8
RWKV_HEAD = 64
RWKV_WIDTH = RWKV_HEADS * RWKV_HEAD
DECAY_LORA = 64
AAA_LORA = 64
GATE_LORA = 128
GN_EPS = 64e-5
RWKV_IN = 3 * RWKV_WIDTH + DECAY_LORA + AAA_LORA + GATE_LORA
D_FF = 2816
NORM_EPS = 1e-6

META_PAD = BLOCK - N_META
CHUNK = 64
DECAY_SCALE = 0.6065306597126334

VMEM_LIMIT = 56 * 1024 * 1024

F32 = jnp.float32
BF16 = jnp.bfloat16
HI = lax.Precision.HIGHEST


def _rms(x, gain):
    ms = jnp.mean(x * x, axis=-1, keepdims=True)
    return x * lax.rsqrt(ms + NORM_EPS) * gain


def _dot(a, b):
    return jnp.dot(a, b, preferred_element_type=F32)


def _dot_hi(a, b):
    return jnp.dot(a, b, preferred_element_type=F32, precision=HI)


def _dot_nt_hi(a, b):
    return lax.dot_general(a, b, (((1,), (1,)), ((), ())),
                           preferred_element_type=F32, precision=HI)


def _dot_tn_hi(a, b):
    return lax.dot_general(a, b, (((0,), (0,)), ((), ())),
                           preferred_element_type=F32, precision=HI)


def _const_spec(shape):
    nd = len(shape)
    return pl.BlockSpec(shape, lambda *_: (0,) * nd, pipeline_mode=pl.Buffered(1))


def _ffn_kernel(h_ref, gpre_ref, wgu_ref, wd_ref, gpost_ref, o_ref):
    h = h_ref[...]
    hn = _rms(h, gpre_ref[...]).astype(BF16)
    gate = _dot(hn, wgu_ref[:, :D_FF])
    up = _dot(hn, wgu_ref[:, D_FF:])
    act = (gate * jax.nn.sigmoid(gate) * up).astype(BF16)
    f = _dot(act, wd_ref[...])
    o_ref[...] = h + 0.5 * _rms(f, gpost_ref[...])


def _ffn(h, gpre, wgu, wd, gpost, tm):
    rows = h.shape[0]
    tm = min(tm, rows)
    return pl.pallas_call(
        _ffn_kernel,
        grid=(rows // tm,),
        in_specs=[
            pl.BlockSpec((tm, D_MODEL), lambda i: (i, 0)),
            _const_spec((1, D_MODEL)),
            _const_spec((D_MODEL, 2 * D_FF)),
            _const_spec((D_FF, D_MODEL)),
            _const_spec((1, D_MODEL)),
        ],
        out_specs=pl.BlockSpec((tm, D_MODEL), lambda i: (i, 0)),
        out_shape=jax.ShapeDtypeStruct((rows, D_MODEL), F32),
        compiler_params=pltpu.CompilerParams(
            dimension_semantics=("parallel",), vmem_limit_bytes=VMEM_LIMIT),
        name="ffn",
    )(h, gpre, wgu, wd, gpost)


C_Q = ATT_Q
C_K = C_Q + ATT_KV
C_V = C_K + ATT_KV
C_R = C_V + RWKV_IN
C_GA = C_R + D_MODEL
IN_COLS = C_GA + D_MODEL


def _rope(x, cos_t, sin_lo, sin_hi):
    n = x.shape[1] // 128
    if n > 1:
        cos_t = jnp.concatenate([cos_t] * n, axis=1)
        sin_lo = jnp.concatenate([sin_lo] * n, axis=1)
        sin_hi = jnp.concatenate([sin_hi] * n, axis=1)
    width = x.shape[1]
    half = ROT_DIM // 2
    from_hi = pltpu.roll(x, width - half, 1)
    from_lo = pltpu.roll(x, half, 1)
    return x * cos_t + from_hi * sin_lo + from_lo * sin_hi


def _inproj_kernel(h_ref, g_ref, w_ref, cos_ref, slo_ref, shi_ref,
                   q_ref, k_ref, v_ref, zr_ref, ga_ref, gr_ref):
    u = _rms(h_ref[...], g_ref[...]).astype(BF16)
    cos_t, sin_lo, sin_hi = cos_ref[...], slo_ref[...], shi_ref[...]
    q = _dot(u, w_ref[:, :C_Q])
    q_ref[...] = (_rope(q, cos_t, sin_lo, sin_hi) * (HEAD_DIM ** -0.5)).astype(BF16)
    k = _dot(u, w_ref[:, C_Q:C_K])
    k_ref[...] = _rope(k, cos_t, sin_lo, sin_hi).astype(BF16)
    v_ref[...] = _dot(u, w_ref[:, C_K:C_V]).astype(BF16)
    zr_ref[...] = _dot(u, w_ref[:, C_V:C_R])
    ga_ref[...] = jax.nn.sigmoid(_dot(u, w_ref[:, C_R:C_GA])).astype(BF16)
    gr_ref[...] = jax.nn.sigmoid(_dot(u, w_ref[:, C_GA:])).astype(BF16)


def _inproj(h, gain, w_in, cos_t, sin_lo, sin_hi, tm):
    rows = h.shape[0]
    tm = min(tm, rows)
    tab_blocks = cos_t.shape[0] // tm

    def row(i):
        return (i, 0)

    def tab(i):
        return (i % tab_blocks, 0)

    widths = (ATT_Q, ATT_KV, ATT_KV, RWKV_IN, D_MODEL, D_MODEL)
    dtypes = (BF16, BF16, BF16, F32, BF16, BF16)
    return pl.pallas_call(
        _inproj_kernel,
        grid=(rows // tm,),
        in_specs=[
            pl.BlockSpec((tm, D_MODEL), row),
            _const_spec((1, D_MODEL)),
            _const_spec((D_MODEL, IN_COLS)),
            pl.BlockSpec((tm, 128), tab),
            pl.BlockSpec((tm, 128), tab),
            pl.BlockSpec((tm, 128), tab),
        ],
        out_specs=[pl.BlockSpec((tm, w), row) for w in widths],
        out_shape=[jax.ShapeDtypeStruct((rows, w), d) for w, d in zip(widths, dtypes)],
        compiler_params=pltpu.CompilerParams(
            dimension_semantics=("parallel",), vmem_limit_bytes=VMEM_LIMIT),
        name="inproj",
    )(h, gain, w_in, cos_t, sin_lo, sin_hi)


def _rope_tables(pos):
    half = ROT_DIM // 2
    inv_freq = 1.0 / (ROPE_THETA ** (jnp.arange(half, dtype=F32) * (2.0 / ROT_DIM)))
    ang = pos[:, None] * inv_freq[None, :]
    cos, sin = jnp.cos(ang), jnp.sin(ang)
    t = pos.shape[0]
    ones = jnp.ones((t, HEAD_DIM - ROT_DIM), F32)
    zeros_h = jnp.zeros((t, half), F32)
    zeros_r = jnp.zeros((t, HEAD_DIM - ROT_DIM), F32)
    cos_t = jnp.concatenate([cos, cos, ones], axis=1)
    sin_lo = jnp.concatenate([-sin, zeros_h, zeros_r], axis=1)
    sin_hi = jnp.concatenate([zeros_h, sin, zeros_r], axis=1)
    return tuple(jnp.concatenate([a, a], axis=1) for a in (cos_t, sin_lo, sin_hi))


def _attn_kernel(sink_ref, q_ref, k_ref, v_ref, kp_ref, vp_ref, km_ref, vm_ref, o_ref):
    n = pl.program_id(1)
    first = n == 0
    q = q_ref[...]
    k_prev = jnp.where(first, km_ref[...], kp_ref[...])
    v_prev = jnp.where(first, vm_ref[...], vp_ref[...])
    k2 = jnp.concatenate([k_prev, k_ref[...]], axis=0)
    v2 = jnp.concatenate([v_prev, v_ref[...]], axis=0)
    row = lax.broadcasted_iota(jnp.int32, (BLOCK, 2 * BLOCK), 0)
    col = lax.broadcasted_iota(jnp.int32, (BLOCK, 2 * BLOCK), 1)
    lo = jnp.where(first, jnp.maximum(row, META_PAD - 1), row)
    mask = (col > lo) & (col <= row + WINDOW)
    outs = []
    for g in range(ATT_KV_HEADS):
        kg = k2[:, g * HEAD_DIM:(g + 1) * HEAD_DIM]
        vg = v2[:, g * HEAD_DIM:(g + 1) * HEAD_DIM]
        for j in range(ATT_GROUP):
            hd = g * ATT_GROUP + j
            qh = q[:, hd * HEAD_DIM:(hd + 1) * HEAD_DIM]
            s = lax.dot_general(qh, kg, (((1,), (1,)), ((), ())), preferred_element_type=F32)
            s = jnp.where(mask, s, MASK_VALUE)
            sink = sink_ref[hd]
            m = jnp.maximum(jnp.max(s, axis=-1, keepdims=True), sink)
            p = jnp.exp(s - m)
            denom = jnp.sum(p, axis=-1, keepdims=True) + jnp.exp(sink - m)
            o = _dot(p.astype(BF16), vg)
            outs.append(o / denom)
    o_ref[...] = jnp.concatenate(outs, axis=1).astype(BF16)


def _attention(sinks, q, k, v, k_meta, v_meta, batch, nblk):
    def own(b, n):
        return (b * nblk + n, 0)

    def prev(b, n):
        return (b * nblk + jnp.maximum(n - 1, 0), 0)

    return pl.pallas_call(
        _attn_kernel,
        grid=(batch, nblk),
        in_specs=[
            pl.BlockSpec(memory_space=pltpu.SMEM),
            pl.BlockSpec((BLOCK, ATT_Q), own),
            pl.BlockSpec((BLOCK, ATT_KV), own),
            pl.BlockSpec((BLOCK, ATT_KV), own),
            pl.BlockSpec((BLOCK, ATT_KV), prev),
            pl.BlockSpec((BLOCK, ATT_KV), prev),
            _const_spec((BLOCK, ATT_KV)),
            _const_spec((BLOCK, ATT_KV)),
        ],
        out_specs=pl.BlockSpec((BLOCK, ATT_Q), own),
        out_shape=jax.ShapeDtypeStruct((batch * nblk * BLOCK, ATT_Q), BF16),
        compiler_params=pltpu.CompilerParams(
            dimension_semantics=("parallel", "parallel"), vmem_limit_bytes=VMEM_LIMIT),
        name="attention",
    )(sinks, q, k, v, k, v, k_meta, v_meta)


O_K = RWKV_WIDTH
O_V = 2 * RWKV_WIDTH
O_W = 3 * RWKV_WIDTH
O_A = O_W + DECAY_LORA
O_G = O_A + AAA_LORA


def _rwkv_kernel(zr_ref, zprev0_ref, s0_ref, tri_ref, hsum_ref, mu_ref, w0_ref, w2_ref,
                 a0_ref, a2_ref, g2_ref, kk_ref, ka_ref, rk_ref, lnw_ref, lnb_ref,
                 y_ref, sout_ref, state_scr, prev_scr, ybuf):
    n = pl.program_id(1)

    @pl.when(n == 0)
    def _():
        state_scr[...] = s0_ref[...]
        prev_scr[...] = zprev0_ref[...]

    z = zr_ref[...]
    rows = z.shape[0]
    row_id = lax.broadcasted_iota(jnp.int32, z.shape, 0)
    z_prev = jnp.where(row_id == 0, prev_scr[...], pltpu.roll(z, 1, 0))
    prev_scr[...] = z[rows - 1:rows, :]
    zs = z + (z_prev - z) * mu_ref[...]

    r = zs[:, :O_K]
    k = zs[:, O_K:O_V]
    v = zs[:, O_V:O_W]
    xw = zs[:, O_W:O_A]
    xa = zs[:, O_A:O_G]
    xg = zs[:, O_G:]

    hsum = hsum_ref[...]
    ld = -DECAY_SCALE * jax.nn.sigmoid(w0_ref[...] + _dot_hi(jnp.tanh(xw), w2_ref[...]))
    a = jax.nn.sigmoid(a0_ref[...] + _dot_hi(xa, a2_ref[...]))
    g = _dot_hi(jax.nn.sigmoid(xg), g2_ref[...])
    kk = k * kk_ref[...]
    kk = kk / jnp.maximum(jnp.sqrt(_dot_hi(kk * kk, hsum)), 1e-12)
    k = k * (1.0 + (a - 1.0) * ka_ref[...])

    cum = _dot_hi(tri_ref[...], ld)
    e_pos = jnp.exp(cum)
    e_neg = jnp.exp(-cum)
    r_t = r * e_pos
    k_t = k * e_neg
    a_t = -kk * jnp.exp(cum - ld)
    b_t = kk * a * e_neg

    ci = lax.broadcasted_iota(jnp.int32, (CHUNK, CHUNK), 0)
    cj = lax.broadcasted_iota(jnp.int32, (CHUNK, CHUNK), 1)
    strict = ci > cj
    incl = ci >= cj
    eye = (ci == cj).astype(F32)

    for c in range(rows // CHUNK):
        rs = slice(c * CHUNK, (c + 1) * CHUNK)
        last = (c + 1) * CHUNK - 1
        for hd in range(RWKV_HEADS):
            ls = slice(hd * RWKV_HEAD, (hd + 1) * RWKV_HEAD)
            rh, kh, vh, ah, bh = r_t[rs, ls], k_t[rs, ls], v[rs, ls], a_t[rs, ls], b_t[rs, ls]
            w_end = e_pos[last:last + 1, ls]
            m_ab = jnp.where(strict, _dot_nt_hi(ah, bh), 0.0)
            m_ak = jnp.where(strict, _dot_nt_hi(ah, kh), 0.0)
            m_rb = jnp.where(incl, _dot_nt_hi(rh, bh), 0.0)
            m_rk = jnp.where(incl, _dot_nt_hi(rh, kh), 0.0)
            inv = eye + m_ab
            pw = m_ab
            for _ in range(5):
                pw = _dot_hi(pw, pw)
                inv = inv + _dot_hi(inv, pw)
            s_prev = state_scr[hd]
            u = _dot_hi(inv, _dot_nt_hi(ah, s_prev) + _dot_hi(m_ak, vh))
            y = _dot_nt_hi(rh, s_prev) + _dot_hi(m_rb, u) + _dot_hi(m_rk, vh)
            state_scr[hd] = (s_prev + _dot_tn_hi(u, bh) + _dot_tn_hi(vh, kh)) * w_end
            ybuf[rs, ls] = y

    y = ybuf[...]
    mean = _dot_hi(y, hsum) * (1.0 / RWKV_HEAD)
    yc = y - mean
    var = _dot_hi(yc * yc, hsum) * (1.0 / RWKV_HEAD)
    yn = yc * lax.rsqrt(var + GN_EPS) * lnw_ref[...] + lnb_ref[...]
    bonus = _dot_hi(r * k * rk_ref[...], hsum) * v
    y_ref[...] = ((yn + bonus) * g).astype(BF16)
    sout_ref[...] = state_scr[...]


def _rwkv(zr, zprev0, s0, tri, hsum, params, batch, nblk):
    rows = batch * nblk * BLOCK

    def blk(b, n):
        return (b * nblk + n, 0)

    n_par = len(params)
    return pl.pallas_call(
        _rwkv_kernel,
        grid=(batch, nblk),
        in_specs=[
            pl.BlockSpec((BLOCK, RWKV_IN), blk),
            _const_spec((1, RWKV_IN)),
            _const_spec((RWKV_HEADS, RWKV_HEAD, RWKV_HEAD)),
            _const_spec((BLOCK, BLOCK)),
            _const_spec((RWKV_WIDTH, RWKV_WIDTH)),
        ] + [_const_spec(p.shape) for p in params],
        out_specs=[
            pl.BlockSpec((BLOCK, RWKV_WIDTH), blk),
            pl.BlockSpec((None, RWKV_HEADS, RWKV_HEAD, RWKV_HEAD), lambda b, n: (b, 0, 0, 0)),
        ],
        out_shape=[
            jax.ShapeDtypeStruct((rows, RWKV_WIDTH), BF16),
            jax.ShapeDtypeStruct((batch, RWKV_HEADS, RWKV_HEAD, RWKV_HEAD), F32),
        ],
        scratch_shapes=[
            pltpu.VMEM((RWKV_HEADS, RWKV_HEAD, RWKV_HEAD), F32),
            pltpu.VMEM((1, RWKV_IN), F32),
            pltpu.VMEM((BLOCK, RWKV_WIDTH), F32),
        ],
        compiler_params=pltpu.CompilerParams(
            dimension_semantics=("arbitrary", "arbitrary"), vmem_limit_bytes=VMEM_LIMIT),
        name="rwkv",
    )(zr, zprev0, s0, tri, hsum, *params)


def _merge_kernel(h_ref, oa_ref, yr_ref, ga_ref, gr_ref, wa_ref, wr_ref, wo_ref, g_ref, o_ref):
    y_att = _dot(oa_ref[...], wa_ref[...])
    y_rwkv = _dot(yr_ref[...], wr_ref[...])
    merged = ga_ref[...].astype(F32) * y_att + gr_ref[...].astype(F32) * y_rwkv
    m = _dot(merged.astype(BF16), wo_ref[...])
    o_ref[...] = h_ref[...] + _rms(m, g_ref[...])


def _merge(h, o_att, y_rwkv, ga, gr, wa, wr, wo, gain, tm):
    rows = h.shape[0]

    def row(i):
        return (i, 0)

    return pl.pallas_call(
        _merge_kernel,
        grid=(rows // tm,),
        in_specs=[
            pl.BlockSpec((tm, D_MODEL), row),
            pl.BlockSpec((tm, ATT_Q), row),
            pl.BlockSpec((tm, RWKV_WIDTH), row),
            pl.BlockSpec((tm, D_MODEL), row),
            pl.BlockSpec((tm, D_MODEL), row),
            _const_spec((ATT_Q, D_MODEL)),
            _const_spec((RWKV_WIDTH, D_MODEL)),
            _const_spec((D_MODEL, D_MODEL)),
            _const_spec((1, D_MODEL)),
        ],
        out_specs=pl.BlockSpec((tm, D_MODEL), row),
        out_shape=jax.ShapeDtypeStruct((rows, D_MODEL), F32),
        compiler_params=pltpu.CompilerParams(
            dimension_semantics=("parallel",), vmem_limit_bytes=VMEM_LIMIT),
        name="merge",
    )(h, o_att, y_rwkv, ga, gr, wa, wr, wo, gain)


def kernel(x, meta_tokens, ffn1_norm_pre, ffn1_w_gate_up, ffn1_w_down, ffn1_norm_post, mix_norm_pre, w_in, att_sinks, rwkv_mu, rwkv_w0, rwkv_w2, rwkv_a0, rwkv_a2, rwkv_g2, rwkv_k_k, rwkv_k_a, rwkv_r_k, rwkv_ln_w, rwkv_ln_b, w_att_branch, w_rwkv_branch, w_mix_out, mix_norm_post, ffn2_norm_pre, ffn2_w_gate_up, ffn2_w_down, ffn2_norm_post):
    batch, seq, _ = x.shape
    nblk = seq // BLOCK
    depth = w_in.shape[0]
    assert depth == 1 and seq % BLOCK == 0

    tab_x = _rope_tables(jnp.arange(N_META, N_META + seq, dtype=jnp.int32).astype(F32))
    pos_m = jnp.maximum(jnp.arange(BLOCK, dtype=jnp.int32) - META_PAD, 0).astype(F32)
    tab_m = _rope_tables(pos_m)
    ri = jnp.arange(BLOCK)
    tri = ((ri[:, None] >= ri[None, :]) &
           (ri[:, None] // CHUNK == ri[None, :] // CHUNK)).astype(F32)
    li = jnp.arange(RWKV_WIDTH) // RWKV_HEAD
    hsum = (li[:, None] == li[None, :]).astype(F32)

    hx = x.reshape(batch * seq, D_MODEL)
    hm = jnp.concatenate([jnp.zeros((META_PAD, D_MODEL), x.dtype),
                          meta_tokens.astype(x.dtype)], axis=0)

    def row(p):
        return p.reshape(1, -1)

    for l in range(depth):
        wgu1, wd1 = ffn1_w_gate_up[l].astype(BF16), ffn1_w_down[l].astype(BF16)
        wgu2, wd2 = ffn2_w_gate_up[l].astype(BF16), ffn2_w_down[l].astype(BF16)
        w_in_l = w_in[l].astype(BF16)
        wa, wr, wo = (w_att_branch[l].astype(BF16), w_rwkv_branch[l].astype(BF16),
                      w_mix_out[l].astype(BF16))
        rw_params = (row(rwkv_mu[l]), row(rwkv_w0[l]), rwkv_w2[l], row(rwkv_a0[l]), rwkv_a2[l],
                     rwkv_g2[l], row(rwkv_k_k[l]), row(rwkv_k_a[l]), row(rwkv_r_k[l]),
                     row(rwkv_ln_w[l]), row(rwkv_ln_b[l]))

        outs = []
        for h, tabs in ((hm, tab_m), (hx, tab_x)):
            h1 = _ffn(h, row(ffn1_norm_pre[l]), wgu1, wd1, row(ffn1_norm_post[l]), 512)
            outs.append((h1,) + tuple(_inproj(h1, row(mix_norm_pre[l]), w_in_l, *tabs, 512)))
        (hm1, _, km, vm, zrm, _, _), (hx1, q, k, v, zr, ga, gr) = outs

        o_att = _attention(att_sinks[l], q, k, v, km, vm, batch, nblk)

        zero_state = jnp.zeros((RWKV_HEADS, RWKV_HEAD, RWKV_HEAD), F32)
        _, s_meta = _rwkv(zrm, jnp.zeros((1, RWKV_IN), F32), zero_state, tri, hsum,
                          rw_params, 1, 1)
        y_rwkv, _ = _rwkv(zr, zrm[BLOCK - 1:], s_meta[0], tri, hsum, rw_params, batch, nblk)

        hx2 = _merge(hx1, o_att, y_rwkv, ga, gr, wa, wr, wo, row(mix_norm_post[l]), 512)
        hx = _ffn(hx2, row(ffn2_norm_pre[l]), wgu2, wd2, row(ffn2_norm_post[l]), 512)
    return hx.reshape(batch, seq, D_MODEL)
```

```python
import functools

import jax
import jax.numpy as jnp
from jax import lax
from jax.experimental import pallas as pl
from jax.experimental.pallas import tpu as pltpu

D_MODEL = 1024
N_META = 16
ATT_HEADS = 8
ATT_KV_HEADS = 2
ATT_GROUP = ATT_HEADS // ATT_KV_HEADS
HEAD_DIM = 64
WINDOW = 128
BLOCK = 128
ROPE_THETA = 500000.0
ROT_DIM = HEAD_DIM // 4
ATT_Q = ATT_HEADS * HEAD_DIM
ATT_KV = ATT_KV_HEADS * HEAD_DIM
MASK_VALUE = -1e30
RWKV_HEADS = 8
RWKV_HEAD = 64
RWKV_WIDTH = RWKV_HEADS * RWKV_HEAD
DECAY_LORA = 64
AAA_LORA = 64
GATE_LORA = 128
GN_EPS = 64e-5
RWKV_IN = 3 * RWKV_WIDTH + DECAY_LORA + AAA_LORA + GATE_LORA
D_FF = 2816
NORM_EPS = 1e-6

META_PAD = BLOCK - N_META
CHUNK = 64
DECAY_SCALE = 0.6065306597126334

VMEM_LIMIT = 56 * 1024 * 1024

F32 = jnp.float32
BF16 = jnp.bfloat16
HI = lax.Precision.HIGHEST


def _rms(x, gain):
    ms = jnp.mean(x * x, axis=-1, keepdims=True)
    return x * lax.rsqrt(ms + NORM_EPS) * gain


def _dot(a, b):
    return jnp.dot(a, b, preferred_element_type=F32)


def _dot_hi(a, b):
    return jnp.dot(a, b, preferred_element_type=F32, precision=HI)


def _dot_nt_hi(a, b):
    return lax.dot_general(a, b, (((1,), (1,)), ((), ())),
                           preferred_element_type=F32, precision=HI)


def _dot_tn_hi(a, b):
    return lax.dot_general(a, b, (((0,), (0,)), ((), ())),
                           preferred_element_type=F32, precision=HI)


def _const_spec(shape):
    nd = len(shape)
    return pl.BlockSpec(shape, lambda *_: (0,) * nd, pipeline_mode=pl.Buffered(1))


def _ffn_kernel(h_ref, gpre_ref, wgu_ref, wd_ref, gpost_ref, o_ref):
    h = h_ref[...]
    hn = _rms(h, gpre_ref[...]).astype(BF16)
    gate = _dot(hn, wgu_ref[:, :D_FF])
    up = _dot(hn, wgu_ref[:, D_FF:])
    act = (gate * jax.nn.sigmoid(gate) * up).astype(BF16)
    f = _dot(act, wd_ref[...])
    o_ref[...] = h + 0.5 * _rms(f, gpost_ref[...])


def _ffn(h, gpre, wgu, wd, gpost, tm):
    rows = h.shape[0]
    tm = min(tm, rows)
    return pl.pallas_call(
        _ffn_kernel,
        grid=(rows // tm,),
        in_specs=[
            pl.BlockSpec((tm, D_MODEL), lambda i: (i, 0)),
            _const_spec((1, D_MODEL)),
            _const_spec((D_MODEL, 2 * D_FF)),
            _const_spec((D_FF, D_MODEL)),
            _const_spec((1, D_MODEL)),
        ],
        out_specs=pl.BlockSpec((tm, D_MODEL), lambda i: (i, 0)),
        out_shape=jax.ShapeDtypeStruct((rows, D_MODEL), F32),
        compiler_params=pltpu.CompilerParams(
            dimension_semantics=("parallel",), vmem_limit_bytes=VMEM_LIMIT),
        name="ffn",
    )(h, gpre, wgu, wd, gpost)


C_Q = ATT_Q
C_K = C_Q + ATT_KV
C_V = C_K + ATT_KV
C_R = C_V + RWKV_IN
C_GA = C_R + D_MODEL
IN_COLS = C_GA + D_MODEL


def _rope(x, cos_t, sin_lo, sin_hi):
    n = x.shape[1] // 128
    if n > 1:
        cos_t = jnp.concatenate([cos_t] * n, axis=1)
        sin_lo = jnp.concatenate([sin_lo] * n, axis=1)
        sin_hi = jnp.concatenate([sin_hi] * n, axis=1)
    width = x.shape[1]
    half = ROT_DIM // 2
    from_hi = pltpu.roll(x, width - half, 1)
    from_lo = pltpu.roll(x, half, 1)
    return x * cos_t + from_hi * sin_lo + from_lo * sin_hi


def _inproj_kernel(h_ref, g_ref, w_ref, cos_ref, slo_ref, shi_ref,
                   q_ref, k_ref, v_ref, zr_ref, ga_ref, gr_ref):
    u = _rms(h_ref[...], g_ref[...]).astype(BF16)
    cos_t, sin_lo, sin_hi = cos_ref[...], slo_ref[...], shi_ref[...]
    q = _dot(u, w_ref[:, :C_Q])
    q_ref[...] = (_rope(q, cos_t, sin_lo, sin_hi) * (HEAD_DIM ** -0.5)).astype(BF16)
    k = _dot(u, w_ref[:, C_Q:C_K])
    k_ref[...] = _rope(k, cos_t, sin_lo, sin_hi).astype(BF16)
    v_ref[...] = _dot(u, w_ref[:, C_K:C_V]).astype(BF16)
    zr_ref[...] = _dot(u, w_ref[:, C_V:C_R])
    ga_ref[...] = jax.nn.sigmoid(_dot(u, w_ref[:, C_R:C_GA])).astype(BF16)
    gr_ref[...] = jax.nn.sigmoid(_dot(u, w_ref[:, C_GA:])).astype(BF16)


def _inproj(h, gain, w_in, cos_t, sin_lo, sin_hi, tm):
    rows = h.shape[0]
    tm = min(tm, rows)
    tab_blocks = cos_t.shape[0] // tm

    def row(i):
        return (i, 0)

    def tab(i):
        return (i % tab_blocks, 0)

    widths = (ATT_Q, ATT_KV, ATT_KV, RWKV_IN, D_MODEL, D_MODEL)
    dtypes = (BF16, BF16, BF16, F32, BF16, BF16)
    return pl.pallas_call(
        _inproj_kernel,
        grid=(rows // tm,),
        in_specs=[
            pl.BlockSpec((tm, D_MODEL), row),
            _const_spec((1, D_MODEL)),
            _const_spec((D_MODEL, IN_COLS)),
            pl.BlockSpec((tm, 128), tab),
            pl.BlockSpec((tm, 128), tab),
            pl.BlockSpec((tm, 128), tab),
        ],
        out_specs=[pl.BlockSpec((tm, w), row) for w in widths],
        out_shape=[jax.ShapeDtypeStruct((rows, w), d) for w, d in zip(widths, dtypes)],
        compiler_params=pltpu.CompilerParams(
            dimension_semantics=("parallel",), vmem_limit_bytes=VMEM_LIMIT),
        name="inproj",
    )(h, gain, w_in, cos_t, sin_lo, sin_hi)


def _rope_tables(pos):
    half = ROT_DIM // 2
    inv_freq = 1.0 / (ROPE_THETA ** (jnp.arange(half, dtype=F32) * (2.0 / ROT_DIM)))
    ang = pos[:, None] * inv_freq[None, :]
    cos, sin = jnp.cos(ang), jnp.sin(ang)
    t = pos.shape[0]
    ones = jnp.ones((t, HEAD_DIM - ROT_DIM), F32)
    zeros_h = jnp.zeros((t, half), F32)
    zeros_r = jnp.zeros((t, HEAD_DIM - ROT_DIM), F32)
    cos_t = jnp.concatenate([cos, cos, ones], axis=1)
    sin_lo = jnp.concatenate([-sin, zeros_h, zeros_r], axis=1)
    sin_hi = jnp.concatenate([zeros_h, sin, zeros_r], axis=1)
    return tuple(jnp.concatenate([a, a], axis=1) for a in (cos_t, sin_lo, sin_hi))


def _attn_kernel(sink_ref, q_ref, k_ref, v_ref, kp_ref, vp_ref, km_ref, vm_ref, o_ref):
    n = pl.program_id(1)
    first = n == 0
    q = q_ref[...]
    k_prev = jnp.where(first, km_ref[...], kp_ref[...])
    v_prev = jnp.where(first, vm_ref[...], vp_ref[...])
    k2 = jnp.concatenate([k_prev, k_ref[...]], axis=0)
    v2 = jnp.concatenate([v_prev, v_ref[...]], axis=0)
    row = lax.broadcasted_iota(jnp.int32, (BLOCK, 2 * BLOCK), 0)
    col = lax.broadcasted_iota(jnp.int32, (BLOCK, 2 * BLOCK), 1)
    lo = jnp.where(first, jnp.maximum(row, META_PAD - 1), row)
    mask = (col > lo) & (col <= row + WINDOW)
    outs = []
    for g in range(ATT_KV_HEADS):
        kg = k2[:, g * HEAD_DIM:(g + 1) * HEAD_DIM]
        vg = v2[:, g * HEAD_DIM:(g + 1) * HEAD_DIM]
        for j in range(ATT_GROUP):
            hd = g * ATT_GROUP + j
            qh = q[:, hd * HEAD_DIM:(hd + 1) * HEAD_DIM]
            s = lax.dot_general(qh, kg, (((1,), (1,)), ((), ())), preferred_element_type=F32)
            s = jnp.where(mask, s, MASK_VALUE)
            sink = sink_ref[hd]
            m = jnp.maximum(jnp.max(s, axis=-1, keepdims=True), sink)
            p = jnp.exp(s - m)
            denom = jnp.sum(p, axis=-1, keepdims=True) + jnp.exp(sink - m)
            o = _dot(p.astype(BF16), vg)
            outs.append(o / denom)
    o_ref[...] = jnp.concatenate(outs, axis=1).astype(BF16)


def _attention(sinks, q, k, v, k_meta, v_meta, batch, nblk):
    def own(b, n):
        return (b * nblk + n, 0)

    def prev(b, n):
        return (b * nblk + jnp.maximum(n - 1, 0), 0)

    return pl.pallas_call(
        _attn_kernel,
        grid=(batch, nblk),
        in_specs=[
            pl.BlockSpec(memory_space=pltpu.SMEM),
            pl.BlockSpec((BLOCK, ATT_Q), own),
            pl.BlockSpec((BLOCK, ATT_KV), own),
            pl.BlockSpec((BLOCK, ATT_KV), own),
            pl.BlockSpec((BLOCK, ATT_KV), prev),
            pl.BlockSpec((BLOCK, ATT_KV), prev),
            _const_spec((BLOCK, ATT_KV)),
            _const_spec((BLOCK, ATT_KV)),
        ],
        out_specs=pl.BlockSpec((BLOCK, ATT_Q), own),
        out_shape=jax.ShapeDtypeStruct((batch * nblk * BLOCK, ATT_Q), BF16),
        compiler_params=pltpu.CompilerParams(
            dimension_semantics=("parallel", "parallel"), vmem_limit_bytes=VMEM_LIMIT),
        name="attention",
    )(sinks, q, k, v, k, v, k_meta, v_meta)


O_K = RWKV_WIDTH
O_V = 2 * RWKV_WIDTH
O_W = 3 * RWKV_WIDTH
O_A = O_W + DECAY_LORA
O_G = O_A + AAA_LORA


GROUP_HEADS = 4
GROUP_W = GROUP_HEADS * RWKV_HEAD
N_GROUPS = RWKV_HEADS // GROUP_HEADS
CUMSUM_PARTS = 2


def _mm(a, b):
    return jnp.dot(a.astype(BF16), b.astype(BF16), preferred_element_type=F32)


def _mm_nt(a, b):
    return lax.dot_general(a.astype(BF16), b.astype(BF16), (((1,), (1,)), ((), ())),
                           preferred_element_type=F32)


def _mm_tn(a, b):
    return lax.dot_general(a, b, (((0,), (0,)), ((), ())), preferred_element_type=F32)


def _block_diag(x, bd_mask):
    xb = x.astype(BF16)
    reps = bd_mask.shape[0] // x.shape[0]
    return jnp.concatenate([xb] * reps, axis=0) * bd_mask


def _head_sum(x, bd_mask):
    return jnp.concatenate(
        [_mm(x[:, i * GROUP_W:(i + 1) * GROUP_W], bd_mask) for i in range(N_GROUPS)], axis=1)


def _chunk_cumsum(tri, x):
    pieces, rem = [], x
    for i in range(CUMSUM_PARTS):
        hi = rem.astype(BF16)
        pieces.append(hi)
        if i + 1 < CUMSUM_PARTS:
            rem = rem - hi.astype(F32)
    out = jnp.dot(tri, jnp.concatenate(pieces, axis=1), preferred_element_type=F32)
    w = x.shape[1]
    return sum(out[:, i * w:(i + 1) * w] for i in range(CUMSUM_PARTS))


def _rwkv_kernel(zr_ref, zprev0_ref, s0_ref, tri_ref, bd_ref, mu_ref, w0_ref, w2_ref,
                 a0_ref, a2_ref, g2_ref, kk_ref, ka_ref, rk_ref, lnw_ref, lnb_ref,
                 y_ref, sout_ref, state_scr, prev_scr, ybuf):
    n = pl.program_id(1)

    @pl.when(n == 0)
    def _():
        state_scr[...] = s0_ref[...]
        prev_scr[...] = zprev0_ref[...]

    z = zr_ref[...]
    rows = z.shape[0]
    row_id = lax.broadcasted_iota(jnp.int32, z.shape, 0)
    z_prev = jnp.where(row_id == 0, prev_scr[...], pltpu.roll(z, 1, 0))
    prev_scr[...] = z[rows - 1:rows, :]
    zs = z + (z_prev - z) * mu_ref[...]

    r = zs[:, :O_K]
    k = zs[:, O_K:O_V]
    v = zs[:, O_V:O_W]
    xw = zs[:, O_W:O_A]
    xa = zs[:, O_A:O_G]
    xg = zs[:, O_G:]

    bd_mask = bd_ref[...]
    ld = -DECAY_SCALE * jax.nn.sigmoid(w0_ref[...] + _mm(jnp.tanh(xw), w2_ref[...]))
    a = jax.nn.sigmoid(a0_ref[...] + _mm(xa, a2_ref[...]))
    g = _mm(jax.nn.sigmoid(xg), g2_ref[...])
    kk = k * kk_ref[...]
    kk = kk / jnp.maximum(jnp.sqrt(_head_sum(kk * kk, bd_mask)), 1e-12)
    k = k * (1.0 + (a - 1.0) * ka_ref[...])

    cum = _chunk_cumsum(tri_ref[...], ld)
    e_pos = jnp.exp(cum)
    e_neg = jnp.exp(-cum)
    r_t = r * e_pos
    k_t = k * e_neg
    a_t = -kk * jnp.exp(cum - ld)
    b_t = kk * a * e_neg

    ti = lax.broadcasted_iota(jnp.int32, (CHUNK, GROUP_W), 0)
    si = lax.broadcasted_iota(jnp.int32, (CHUNK, GROUP_W), 1) % CHUNK
    strict = ti > si
    incl = ti >= si
    eye = (ti == si).astype(F32)

    for c in range(rows // CHUNK):
        rs = slice(c * CHUNK, (c + 1) * CHUNK)
        last = (c + 1) * CHUNK - 1
        for gi in range(N_GROUPS):
            ls = slice(gi * GROUP_W, (gi + 1) * GROUP_W)
            rg, kg, vg, ag, bg = r_t[rs, ls], k_t[rs, ls], v[rs, ls], a_t[rs, ls], b_t[rs, ls]
            w_end = e_pos[last:last + 1, ls]
            ar = jnp.concatenate([ag, rg], axis=0)
            bd_bk = jnp.concatenate([_block_diag(bg, bd_mask), _block_diag(kg, bd_mask)], axis=0)
            sc = _mm_nt(ar, bd_bk)
            m_ab = jnp.where(strict, sc[:CHUNK, :GROUP_W], 0.0)
            m_ak = jnp.where(strict, sc[:CHUNK, GROUP_W:], 0.0)
            m_rb = jnp.where(incl, sc[CHUNK:, :GROUP_W], 0.0)
            m_rk = jnp.where(incl, sc[CHUNK:, GROUP_W:], 0.0)
            inv = eye + m_ab
            pw = _mm(m_ab, _block_diag(m_ab, bd_mask))
            for _ in range(4):
                both = _mm(jnp.concatenate([pw, inv], axis=0), _block_diag(pw, bd_mask))
                pw = both[:CHUNK]
                inv = inv + both[CHUNK:]
            inv = inv + _mm(inv, _block_diag(pw, bd_mask))
            s_prev = state_scr[gi]
            xr0 = _mm_nt(ar, s_prev)
            mv = _mm(jnp.concatenate([m_ak, m_rk], axis=0), _block_diag(vg, bd_mask))
            u = _mm(inv, _block_diag(xr0[:CHUNK] + mv[:CHUNK], bd_mask))
            y = xr0[CHUNK:] + mv[CHUNK:] + _mm(m_rb, _block_diag(u, bd_mask))
            upd = _mm_tn(jnp.concatenate([u, vg], axis=0), jnp.concatenate([bg, kg], axis=0))
            state_scr[gi] = (s_prev + upd * bd_mask.astype(F32)) * w_end
            ybuf[rs, ls] = y

    y = ybuf[...]
    mean = _head_sum(y, bd_mask) * (1.0 / RWKV_HEAD)
    yc = y - mean
    var = _head_sum(yc * yc, bd_mask) * (1.0 / RWKV_HEAD)
    yn = yc * lax.rsqrt(var + GN_EPS) * lnw_ref[...] + lnb_ref[...]
    bonus = _head_sum(r * k * rk_ref[...], bd_mask) * v
    y_ref[...] = ((yn + bonus) * g).astype(BF16)
    sout_ref[...] = state_scr[...]


def _rwkv(zr, zprev0, s0, tri, bd_mask, params, batch, nblk):
    rows = batch * nblk * BLOCK
    state_shape = (N_GROUPS, GROUP_W, GROUP_W)

    def blk(b, n):
        return (b * nblk + n, 0)

    return pl.pallas_call(
        _rwkv_kernel,
        grid=(batch, nblk),
        in_specs=[
            pl.BlockSpec((BLOCK, RWKV_IN), blk),
            _const_spec((1, RWKV_IN)),
            _const_spec(state_shape),
            _const_spec((BLOCK, BLOCK)),
            _const_spec((GROUP_W, GROUP_W)),
        ] + [_const_spec(p.shape) for p in params],
        out_specs=[
            pl.BlockSpec((BLOCK, RWKV_WIDTH), blk),
            pl.BlockSpec((None,) + state_shape, lambda b, n: (b, 0, 0, 0)),
        ],
        out_shape=[
            jax.ShapeDtypeStruct((rows, RWKV_WIDTH), BF16),
            jax.ShapeDtypeStruct((batch,) + state_shape, F32),
        ],
        scratch_shapes=[
            pltpu.VMEM(state_shape, F32),
            pltpu.VMEM((1, RWKV_IN), F32),
            pltpu.VMEM((BLOCK, RWKV_WIDTH), F32),
        ],
        compiler_params=pltpu.CompilerParams(
            dimension_semantics=("arbitrary", "arbitrary"), vmem_limit_bytes=VMEM_LIMIT),
        name="rwkv",
    )(zr, zprev0, s0, tri, bd_mask, *params)


def _merge_kernel(h_ref, oa_ref, yr_ref, ga_ref, gr_ref, wa_ref, wr_ref, wo_ref, g_ref, o_ref):
    y_att = _dot(oa_ref[...], wa_ref[...])
    y_rwkv = _dot(yr_ref[...], wr_ref[...])
    merged = ga_ref[...].astype(F32) * y_att + gr_ref[...].astype(F32) * y_rwkv
    m = _dot(merged.astype(BF16), wo_ref[...])
    o_ref[...] = h_ref[...] + _rms(m, g_ref[...])


def _merge(h, o_att, y_rwkv, ga, gr, wa, wr, wo, gain, tm):
    rows = h.shape[0]

    def row(i):
        return (i, 0)

    return pl.pallas_call(
        _merge_kernel,
        grid=(rows // tm,),
        in_specs=[
            pl.BlockSpec((tm, D_MODEL), row),
            pl.BlockSpec((tm, ATT_Q), row),
            pl.BlockSpec((tm, RWKV_WIDTH), row),
            pl.BlockSpec((tm, D_MODEL), row),
            pl.BlockSpec((tm, D_MODEL), row),
            _const_spec((ATT_Q, D_MODEL)),
            _const_spec((RWKV_WIDTH, D_MODEL)),
            _const_spec((D_MODEL, D_MODEL)),
            _const_spec((1, D_MODEL)),
        ],
        out_specs=pl.BlockSpec((tm, D_MODEL), row),
        out_shape=jax.ShapeDtypeStruct((rows, D_MODEL), F32),
        compiler_params=pltpu.CompilerParams(
            dimension_semantics=("parallel",), vmem_limit_bytes=VMEM_LIMIT),
        name="merge",
    )(h, o_att, y_rwkv, ga, gr, wa, wr, wo, gain)


def kernel(x, meta_tokens, ffn1_norm_pre, ffn1_w_gate_up, ffn1_w_down, ffn1_norm_post, mix_norm_pre, w_in, att_sinks, rwkv_mu, rwkv_w0, rwkv_w2, rwkv_a0, rwkv_a2, rwkv_g2, rwkv_k_k, rwkv_k_a, rwkv_r_k, rwkv_ln_w, rwkv_ln_b, w_att_branch, w_rwkv_branch, w_mix_out, mix_norm_post, ffn2_norm_pre, ffn2_w_gate_up, ffn2_w_down, ffn2_norm_post):
    batch, seq, _ = x.shape
    nblk = seq // BLOCK
    depth = w_in.shape[0]
    assert depth == 1 and seq % BLOCK == 0

    tab_x = _rope_tables(jnp.arange(N_META, N_META + seq, dtype=jnp.int32).astype(F32))
    pos_m = jnp.maximum(jnp.arange(BLOCK, dtype=jnp.int32) - META_PAD, 0).astype(F32)
    tab_m = _rope_tables(pos_m)
    ri = jnp.arange(BLOCK)
    tri = ((ri[:, None] >= ri[None, :]) &
           (ri[:, None] // CHUNK == ri[None, :] // CHUNK)).astype(BF16)
    li = jnp.arange(GROUP_W) // RWKV_HEAD
    bd_mask = (li[:, None] == li[None, :]).astype(BF16)

    hx = x.reshape(batch * seq, D_MODEL)
    hm = jnp.concatenate([jnp.zeros((META_PAD, D_MODEL), x.dtype),
                          meta_tokens.astype(x.dtype)], axis=0)

    def row(p):
        return p.reshape(1, -1)

    for l in range(depth):
        wgu1, wd1 = ffn1_w_gate_up[l].astype(BF16), ffn1_w_down[l].astype(BF16)
        wgu2, wd2 = ffn2_w_gate_up[l].astype(BF16), ffn2_w_down[l].astype(BF16)
        w_in_l = w_in[l].astype(BF16)
        wa, wr, wo = (w_att_branch[l].astype(BF16), w_rwkv_branch[l].astype(BF16),
                      w_mix_out[l].astype(BF16))
        rw_params = (row(rwkv_mu[l]), row(rwkv_w0[l]), rwkv_w2[l], row(rwkv_a0[l]), rwkv_a2[l],
                     rwkv_g2[l], row(rwkv_k_k[l]), row(rwkv_k_a[l]), row(rwkv_r_k[l]),
                     row(rwkv_ln_w[l]), row(rwkv_ln_b[l]))

        outs = []
        for h, tabs in ((hm, tab_m), (hx, tab_x)):
            h1 = _ffn(h, row(ffn1_norm_pre[l]), wgu1, wd1, row(ffn1_norm_post[l]), 512)
            outs.append((h1,) + tuple(_inproj(h1, row(mix_norm_pre[l]), w_in_l, *tabs, 512)))
        (hm1, _, km, vm, zrm, _, _), (hx1, q, k, v, zr, ga, gr) = outs

        o_att = _attention(att_sinks[l], q, k, v, km, vm, batch, nblk)

        zero_state = jnp.zeros((N_GROUPS, GROUP_W, GROUP_W), F32)
        _, s_meta = _rwkv(zrm, jnp.zeros((1, RWKV_IN), F32), zero_state, tri, bd_mask,
                          rw_params, 1, 1)
        y_rwkv, _ = _rwkv(zr, zrm[BLOCK - 1:], s_meta[0], tri, bd_mask, rw_params, batch, nblk)

        hx2 = _merge(hx1, o_att, y_rwkv, ga, gr, wa, wr, wo, row(mix_norm_post[l]), 512)
        hx = _ffn(hx2, row(ffn2_norm_pre[l]), wgu2, wd2, row(ffn2_norm_post[l]), 512)
    return hx.reshape(batch, seq, D_MODEL)
```

```python
import jax
import jax.numpy as jnp
from jax import lax
from jax.experimental import pallas as pl
from jax.experimental.pallas import tpu as pltpu

D_MODEL = 1024
N_META = 16
ATT_HEADS = 8
ATT_KV_HEADS = 2
ATT_GROUP = ATT_HEADS // ATT_KV_HEADS
HEAD_DIM = 64
WINDOW = 128
BLOCK = 128
ROPE_THETA = 500000.0
ROT_DIM = HEAD_DIM // 4
ATT_Q = ATT_HEADS * HEAD_DIM
ATT_KV = ATT_KV_HEADS * HEAD_DIM
MASK_VALUE = -1e30
RWKV_HEADS = 8
RWKV_HEAD = 64
RWKV_WIDTH = RWKV_HEADS * RWKV_HEAD
DECAY_LORA = 64
AAA_LORA = 64
GATE_LORA = 128
GN_EPS = 64e-5
RWKV_IN = 3 * RWKV_WIDTH + DECAY_LORA + AAA_LORA + GATE_LORA
D_FF = 2816
NORM_EPS = 1e-6

META_PAD = BLOCK - N_META
CHUNK = 64
DECAY_SCALE = 0.6065306597126334

VMEM_LIMIT = 56 * 1024 * 1024
ROW_TILE = 512

F32 = jnp.float32
BF16 = jnp.bfloat16


def _rms(x, gain):
    ms = jnp.mean(x * x, axis=-1, keepdims=True)
    return x * lax.rsqrt(ms + NORM_EPS) * gain


def _dot(a, b):
    return jnp.dot(a, b, preferred_element_type=F32)


def _const_spec(shape):
    nd = len(shape)
    return pl.BlockSpec(shape, lambda *_: (0,) * nd, pipeline_mode=pl.Buffered(1))


def _ffn_kernel(h_ref, gpre_ref, wgu_ref, wd_ref, gpost_ref, o_ref):
    h = h_ref[...]
    hn = _rms(h, gpre_ref[...]).astype(BF16)
    gate = _dot(hn, wgu_ref[:, :D_FF])
    up = _dot(hn, wgu_ref[:, D_FF:])
    act = (gate * jax.nn.sigmoid(gate) * up).astype(BF16)
    f = _dot(act, wd_ref[...])
    o_ref[...] = h + 0.5 * _rms(f, gpost_ref[...])


def _ffn(h, gpre, wgu, wd, gpost):
    rows = h.shape[0]
    tm = min(ROW_TILE, rows)
    return pl.pallas_call(
        _ffn_kernel,
        grid=(rows // tm,),
        in_specs=[
            pl.BlockSpec((tm, D_MODEL), lambda i: (i, 0)),
            _const_spec((1, D_MODEL)),
            _const_spec((D_MODEL, 2 * D_FF)),
            _const_spec((D_FF, D_MODEL)),
            _const_spec((1, D_MODEL)),
        ],
        out_specs=pl.BlockSpec((tm, D_MODEL), lambda i: (i, 0)),
        out_shape=jax.ShapeDtypeStruct((rows, D_MODEL), F32),
        compiler_params=pltpu.CompilerParams(
            dimension_semantics=("parallel",), vmem_limit_bytes=VMEM_LIMIT),
        name="ffn",
    )(h, gpre, wgu, wd, gpost)


C_Q = ATT_Q
C_K = C_Q + ATT_KV
C_V = C_K + ATT_KV
C_R = C_V + RWKV_IN
C_GA = C_R + D_MODEL
IN_COLS = C_GA + D_MODEL


def _rope(x, cos_t, sin_lo, sin_hi):
    n = x.shape[1] // 128
    if n > 1:
        cos_t = jnp.concatenate([cos_t] * n, axis=1)
        sin_lo = jnp.concatenate([sin_lo] * n, axis=1)
        sin_hi = jnp.concatenate([sin_hi] * n, axis=1)
    width = x.shape[1]
    half = ROT_DIM // 2
    from_hi = pltpu.roll(x, width - half, 1)
    from_lo = pltpu.roll(x, half, 1)
    return x * cos_t + from_hi * sin_lo + from_lo * sin_hi


def _inproj_kernel(h_ref, g_ref, w_ref, cos_ref, slo_ref, shi_ref,
                   q_ref, k_ref, v_ref, zr_ref, ga_ref, gr_ref):
    u = _rms(h_ref[...], g_ref[...]).astype(BF16)
    cos_t, sin_lo, sin_hi = cos_ref[...], slo_ref[...], shi_ref[...]
    q = _dot(u, w_ref[:, :C_Q])
    q_ref[...] = (_rope(q, cos_t, sin_lo, sin_hi) * (HEAD_DIM ** -0.5)).astype(BF16)
    k = _dot(u, w_ref[:, C_Q:C_K])
    k_ref[...] = _rope(k, cos_t, sin_lo, sin_hi).astype(BF16)
    v_ref[...] = _dot(u, w_ref[:, C_K:C_V]).astype(BF16)
    zr_ref[...] = _dot(u, w_ref[:, C_V:C_R])
    ga_ref[...] = jax.nn.sigmoid(_dot(u, w_ref[:, C_R:C_GA])).astype(BF16)
    gr_ref[...] = jax.nn.sigmoid(_dot(u, w_ref[:, C_GA:])).astype(BF16)


def _inproj(h, gain, w_in, cos_t, sin_lo, sin_hi):
    rows = h.shape[0]
    tm = min(ROW_TILE, rows)
    tab_blocks = cos_t.shape[0] // tm

    def row(i):
        return (i, 0)

    def tab(i):
        return (i % tab_blocks, 0)

    widths = (ATT_Q, ATT_KV, ATT_KV, RWKV_IN, D_MODEL, D_MODEL)
    dtypes = (BF16, BF16, BF16, F32, BF16, BF16)
    return pl.pallas_call(
        _inproj_kernel,
        grid=(rows // tm,),
        in_specs=[
            pl.BlockSpec((tm, D_MODEL), row),
            _const_spec((1, D_MODEL)),
            _const_spec((D_MODEL, IN_COLS)),
            pl.BlockSpec((tm, 128), tab),
            pl.BlockSpec((tm, 128), tab),
            pl.BlockSpec((tm, 128), tab),
        ],
        out_specs=[pl.BlockSpec((tm, w), row) for w in widths],
        out_shape=[jax.ShapeDtypeStruct((rows, w), d) for w, d in zip(widths, dtypes)],
        compiler_params=pltpu.CompilerParams(
            dimension_semantics=("parallel",), vmem_limit_bytes=VMEM_LIMIT),
        name="inproj",
    )(h, gain, w_in, cos_t, sin_lo, sin_hi)


def _rope_tables(pos):
    half = ROT_DIM // 2
    inv_freq = 1.0 / (ROPE_THETA ** (jnp.arange(half, dtype=F32) * (2.0 / ROT_DIM)))
    ang = pos[:, None] * inv_freq[None, :]
    cos, sin = jnp.cos(ang), jnp.sin(ang)
    t = pos.shape[0]
    ones = jnp.ones((t, HEAD_DIM - ROT_DIM), F32)
    zeros_h = jnp.zeros((t, half), F32)
    zeros_r = jnp.zeros((t, HEAD_DIM - ROT_DIM), F32)
    cos_t = jnp.concatenate([cos, cos, ones], axis=1)
    sin_lo = jnp.concatenate([-sin, zeros_h, zeros_r], axis=1)
    sin_hi = jnp.concatenate([zeros_h, sin, zeros_r], axis=1)
    return tuple(jnp.concatenate([a, a], axis=1) for a in (cos_t, sin_lo, sin_hi))


def _attn_kernel(sink_ref, q_ref, k_ref, v_ref, kp_ref, vp_ref, km_ref, vm_ref, o_ref):
    n = pl.program_id(1)
    first = n == 0
    q = q_ref[...]
    k_prev = jnp.where(first, km_ref[...], kp_ref[...])
    v_prev = jnp.where(first, vm_ref[...], vp_ref[...])
    k2 = jnp.concatenate([k_prev, k_ref[...]], axis=0)
    v2 = jnp.concatenate([v_prev, v_ref[...]], axis=0)
    row = lax.broadcasted_iota(jnp.int32, (BLOCK, 2 * BLOCK), 0)
    col = lax.broadcasted_iota(jnp.int32, (BLOCK, 2 * BLOCK), 1)
    lo = jnp.where(first, jnp.maximum(row, META_PAD - 1), row)
    mask = (col > lo) & (col <= row + WINDOW)
    outs = []
    for g in range(ATT_KV_HEADS):
        kg = k2[:, g * HEAD_DIM:(g + 1) * HEAD_DIM]
        vg = v2[:, g * HEAD_DIM:(g + 1) * HEAD_DIM]
        for j in range(ATT_GROUP):
            hd = g * ATT_GROUP + j
            qh = q[:, hd * HEAD_DIM:(hd + 1) * HEAD_DIM]
            s = lax.dot_general(qh, kg, (((1,), (1,)), ((), ())), preferred_element_type=F32)
            s = jnp.where(mask, s, MASK_VALUE)
            sink = sink_ref[hd]
            m = jnp.maximum(jnp.max(s, axis=-1, keepdims=True), sink)
            p = jnp.exp(s - m)
            denom = jnp.sum(p, axis=-1, keepdims=True) + jnp.exp(sink - m)
            o = _dot(p.astype(BF16), vg)
            outs.append(o / denom)
    o_ref[...] = jnp.concatenate(outs, axis=1).astype(BF16)


def _attention(sinks, q, k, v, k_meta, v_meta, batch, nblk):
    def own(b, n):
        return (b * nblk + n, 0)

    def prev(b, n):
        return (b * nblk + jnp.maximum(n - 1, 0), 0)

    return pl.pallas_call(
        _attn_kernel,
        grid=(batch, nblk),
        in_specs=[
            pl.BlockSpec(memory_space=pltpu.SMEM),
            pl.BlockSpec((BLOCK, ATT_Q), own),
            pl.BlockSpec((BLOCK, ATT_KV), own),
            pl.BlockSpec((BLOCK, ATT_KV), own),
            pl.BlockSpec((BLOCK, ATT_KV), prev),
            pl.BlockSpec((BLOCK, ATT_KV), prev),
            _const_spec((BLOCK, ATT_KV)),
            _const_spec((BLOCK, ATT_KV)),
        ],
        out_specs=pl.BlockSpec((BLOCK, ATT_Q), own),
        out_shape=jax.ShapeDtypeStruct((batch * nblk * BLOCK, ATT_Q), BF16),
        compiler_params=pltpu.CompilerParams(
            dimension_semantics=("parallel", "parallel"), vmem_limit_bytes=VMEM_LIMIT),
        name="attention",
    )(sinks, q, k, v, k, v, k_meta, v_meta)


O_K = RWKV_WIDTH
O_V = 2 * RWKV_WIDTH
O_W = 3 * RWKV_WIDTH
O_A = O_W + DECAY_LORA
O_G = O_A + AAA_LORA

GROUP_HEADS = 4
GROUP_W = GROUP_HEADS * RWKV_HEAD
N_GROUPS = RWKV_HEADS // GROUP_HEADS
CUMSUM_PARTS = 2
INV_DOUBLINGS = 4


def _mm(a, b):
    return jnp.dot(a.astype(BF16), b.astype(BF16), preferred_element_type=F32)


def _mm_nt(a, b):
    return lax.dot_general(a.astype(BF16), b.astype(BF16), (((1,), (1,)), ((), ())),
                           preferred_element_type=F32)


def _mm_tn(a, b):
    return lax.dot_general(a, b, (((0,), (0,)), ((), ())), preferred_element_type=F32)


def _block_diag(x, bd_mask):
    xb = x.astype(BF16)
    reps = bd_mask.shape[0] // x.shape[0]
    return jnp.concatenate([xb] * reps, axis=0) * bd_mask


def _head_sum(x, bd_mask):
    return jnp.concatenate(
        [_mm(x[:, i * GROUP_W:(i + 1) * GROUP_W], bd_mask) for i in range(N_GROUPS)], axis=1)


def _chunk_cumsum(tri, x):
    pieces, rem = [], x
    for i in range(CUMSUM_PARTS):
        hi = rem.astype(BF16)
        pieces.append(hi)
        if i + 1 < CUMSUM_PARTS:
            rem = rem - hi.astype(F32)
    out = jnp.dot(tri, jnp.concatenate(pieces, axis=1), preferred_element_type=F32)
    w = x.shape[1]
    return sum(out[:, i * w:(i + 1) * w] for i in range(CUMSUM_PARTS))


def _rwkv_kernel(zr_ref, zprev0_ref, s0_ref, tri_ref, bd_ref, mu_ref, w0_ref, w2_ref,
                 a0_ref, a2_ref, g2_ref, kk_ref, ka_ref, rk_ref, lnw_ref, lnb_ref,
                 y_ref, sout_ref, state_scr, prev_scr, ybuf):
    nb = zr_ref.shape[0]
    n = pl.program_id(0)

    @pl.when(n == 0)
    def _():
        for b in range(nb):
            state_scr[b] = s0_ref[...]
            prev_scr[b] = zprev0_ref[...]

    row_id = lax.broadcasted_iota(jnp.int32, (BLOCK, RWKV_IN), 0)
    shifted = []
    for b in range(nb):
        z = zr_ref[b]
        z_prev = jnp.where(row_id == 0, prev_scr[b], pltpu.roll(z, 1, 0))
        prev_scr[b] = z[BLOCK - 1:BLOCK, :]
        shifted.append(z + (z_prev - z) * mu_ref[...])
    zs = jnp.concatenate(shifted, axis=0)

    r = zs[:, :O_K]
    k = zs[:, O_K:O_V]
    v = zs[:, O_V:O_W]
    xw = zs[:, O_W:O_A]
    xa = zs[:, O_A:O_G]
    xg = zs[:, O_G:]

    bd_mask = bd_ref[...]
    ld = -DECAY_SCALE * jax.nn.sigmoid(w0_ref[...] + _mm(jnp.tanh(xw), w2_ref[...]))
    a = jax.nn.sigmoid(a0_ref[...] + _mm(xa, a2_ref[...]))
    g = _mm(jax.nn.sigmoid(xg), g2_ref[...])
    kk = k * kk_ref[...]
    kk = kk / jnp.maximum(jnp.sqrt(_head_sum(kk * kk, bd_mask)), 1e-12)
    k = k * (1.0 + (a - 1.0) * ka_ref[...])

    tri = tri_ref[...]
    cum = jnp.concatenate(
        [_chunk_cumsum(tri, ld[b * BLOCK:(b + 1) * BLOCK]) for b in range(nb)], axis=0)
    e_pos = jnp.exp(cum)
    e_neg = jnp.exp(-cum)
    r_t = r * e_pos
    k_t = k * e_neg
    a_t = -kk * jnp.exp(cum - ld)
    b_t = kk * a * e_neg

    ti = lax.broadcasted_iota(jnp.int32, (CHUNK, GROUP_W), 0)
    si = lax.broadcasted_iota(jnp.int32, (CHUNK, GROUP_W), 1) % CHUNK
    strict = ti > si
    incl = ti >= si
    eye = (ti == si).astype(F32)

    n_chunks = BLOCK // CHUNK
    seqs = [(b, gi) for b in range(nb) for gi in range(N_GROUPS)]
    chains = [(b, c, gi) for c in range(n_chunks) for (b, gi) in seqs]

    def part(x, ch):
        b, c, gi = ch
        r0 = b * BLOCK + c * CHUNK
        return x[r0:r0 + CHUNK, gi * GROUP_W:(gi + 1) * GROUP_W]

    ar, m_ab, m_ak, m_rb, m_rk = {}, {}, {}, {}, {}
    for ch in chains:
        ar[ch] = jnp.concatenate([part(a_t, ch), part(r_t, ch)], axis=0)
        bd_bk = jnp.concatenate([_block_diag(part(b_t, ch), bd_mask),
                                 _block_diag(part(k_t, ch), bd_mask)], axis=0)
        sc = _mm_nt(ar[ch], bd_bk)
        m_ab[ch] = jnp.where(strict, sc[:CHUNK, :GROUP_W], 0.0)
        m_ak[ch] = jnp.where(strict, sc[:CHUNK, GROUP_W:], 0.0)
        m_rb[ch] = jnp.where(incl, sc[CHUNK:, :GROUP_W], 0.0)
        m_rk[ch] = jnp.where(incl, sc[CHUNK:, GROUP_W:], 0.0)

    inv = {ch: eye + m_ab[ch] for ch in chains}
    pw = {ch: _mm(m_ab[ch], _block_diag(m_ab[ch], bd_mask)) for ch in chains}
    for _ in range(INV_DOUBLINGS):
        for ch in chains:
            both = _mm(jnp.concatenate([pw[ch], inv[ch]], axis=0), _block_diag(pw[ch], bd_mask))
            pw[ch] = both[:CHUNK]
            inv[ch] = inv[ch] + both[CHUNK:]
    for ch in chains:
        inv[ch] = inv[ch] + _mm(inv[ch], _block_diag(pw[ch], bd_mask))

    mv = {ch: _mm(jnp.concatenate([m_ak[ch], m_rk[ch]], axis=0), _block_diag(part(v, ch), bd_mask))
          for ch in chains}

    state = {sq: state_scr[sq[0], sq[1]] for sq in seqs}
    bd_f32 = bd_mask.astype(F32)
    for c in range(n_chunks):
        cur = [(b, c, gi) for (b, gi) in seqs]
        xr0 = {ch: _mm_nt(ar[ch], state[(ch[0], ch[2])]) for ch in cur}
        u = {ch: _mm(inv[ch], _block_diag(xr0[ch][:CHUNK] + mv[ch][:CHUNK], bd_mask)) for ch in cur}
        for ch in cur:
            b, _, gi = ch
            y = xr0[ch][CHUNK:] + mv[ch][CHUNK:] + _mm(m_rb[ch], _block_diag(u[ch], bd_mask))
            upd = _mm_tn(jnp.concatenate([u[ch], part(v, ch)], axis=0),
                         jnp.concatenate([part(b_t, ch), part(k_t, ch)], axis=0))
            last = b * BLOCK + (c + 1) * CHUNK - 1
            w_end = e_pos[last:last + 1, gi * GROUP_W:(gi + 1) * GROUP_W]
            state[(b, gi)] = (state[(b, gi)] + upd * bd_f32) * w_end
            ybuf[b, c * CHUNK:(c + 1) * CHUNK, gi * GROUP_W:(gi + 1) * GROUP_W] = y
    for sq in seqs:
        state_scr[sq[0], sq[1]] = state[sq]
    for gi in range(N_GROUPS):
        sout_ref[gi] = state[(0, gi)]

    for b in range(nb):
        rows = slice(b * BLOCK, (b + 1) * BLOCK)
        y = ybuf[b]
        mean = _head_sum(y, bd_mask) * (1.0 / RWKV_HEAD)
        yc = y - mean
        var = _head_sum(yc * yc, bd_mask) * (1.0 / RWKV_HEAD)
        yn = yc * lax.rsqrt(var + GN_EPS) * lnw_ref[...] + lnb_ref[...]
        bonus = _head_sum(r[rows] * k[rows] * rk_ref[...], bd_mask) * v[rows]
        y_ref[b] = ((yn + bonus) * g[rows]).astype(BF16)


def _rwkv(zr, zprev0, s0, tri, bd_mask, params):
    batch, rows, _ = zr.shape
    state_shape = (N_GROUPS, GROUP_W, GROUP_W)
    return pl.pallas_call(
        _rwkv_kernel,
        grid=(rows // BLOCK,),
        in_specs=[
            pl.BlockSpec((batch, BLOCK, RWKV_IN), lambda n: (0, n, 0)),
            _const_spec((1, RWKV_IN)),
            _const_spec(state_shape),
            _const_spec((BLOCK, BLOCK)),
            _const_spec((GROUP_W, GROUP_W)),
        ] + [_const_spec(p.shape) for p in params],
        out_specs=[
            pl.BlockSpec((batch, BLOCK, RWKV_WIDTH), lambda n: (0, n, 0)),
            pl.BlockSpec(state_shape, lambda n: (0, 0, 0)),
        ],
        out_shape=[
            jax.ShapeDtypeStruct((batch, rows, RWKV_WIDTH), BF16),
            jax.ShapeDtypeStruct(state_shape, F32),
        ],
        scratch_shapes=[
            pltpu.VMEM((batch,) + state_shape, F32),
            pltpu.VMEM((batch, 1, RWKV_IN), F32),
            pltpu.VMEM((batch, BLOCK, RWKV_WIDTH), F32),
        ],
        compiler_params=pltpu.CompilerParams(
            dimension_semantics=("arbitrary",), vmem_limit_bytes=VMEM_LIMIT),
        name="rwkv",
    )(zr, zprev0, s0, tri, bd_mask, *params)


def _merge_kernel(h_ref, oa_ref, yr_ref, ga_ref, gr_ref, wa_ref, wr_ref, wo_ref, g_ref, o_ref):
    y_att = _dot(oa_ref[...], wa_ref[...])
    y_rwkv = _dot(yr_ref[...], wr_ref[...])
    merged = ga_ref[...].astype(F32) * y_att + gr_ref[...].astype(F32) * y_rwkv
    m = _dot(merged.astype(BF16), wo_ref[...])
    o_ref[...] = h_ref[...] + _rms(m, g_ref[...])


def _merge(h, o_att, y_rwkv, ga, gr, wa, wr, wo, gain):
    rows = h.shape[0]
    tm = min(ROW_TILE, rows)

    def row(i):
        return (i, 0)

    return pl.pallas_call(
        _merge_kernel,
        grid=(rows // tm,),
        in_specs=[
            pl.BlockSpec((tm, D_MODEL), row),
            pl.BlockSpec((tm, ATT_Q), row),
            pl.BlockSpec((tm, RWKV_WIDTH), row),
            pl.BlockSpec((tm, D_MODEL), row),
            pl.BlockSpec((tm, D_MODEL), row),
            _const_spec((ATT_Q, D_MODEL)),
            _const_spec((RWKV_WIDTH, D_MODEL)),
            _const_spec((D_MODEL, D_MODEL)),
            _const_spec((1, D_MODEL)),
        ],
        out_specs=pl.BlockSpec((tm, D_MODEL), row),
        out_shape=jax.ShapeDtypeStruct((rows, D_MODEL), F32),
        compiler_params=pltpu.CompilerParams(
            dimension_semantics=("parallel",), vmem_limit_bytes=VMEM_LIMIT),
        name="merge",
    )(h, o_att, y_rwkv, ga, gr, wa, wr, wo, gain)


def kernel(x, meta_tokens, ffn1_norm_pre, ffn1_w_gate_up, ffn1_w_down, ffn1_norm_post, mix_norm_pre, w_in, att_sinks, rwkv_mu, rwkv_w0, rwkv_w2, rwkv_a0, rwkv_a2, rwkv_g2, rwkv_k_k, rwkv_k_a, rwkv_r_k, rwkv_ln_w, rwkv_ln_b, w_att_branch, w_rwkv_branch, w_mix_out, mix_norm_post, ffn2_norm_pre, ffn2_w_gate_up, ffn2_w_down, ffn2_norm_post):
    batch, seq, _ = x.shape
    nblk = seq // BLOCK
    depth = w_in.shape[0]
    assert depth == 1 and seq % BLOCK == 0

    tab_x = _rope_tables(jnp.arange(N_META, N_META + seq, dtype=jnp.int32).astype(F32))
    pos_m = jnp.maximum(jnp.arange(BLOCK, dtype=jnp.int32) - META_PAD, 0).astype(F32)
    tab_m = _rope_tables(pos_m)
    ri = jnp.arange(BLOCK)
    tri = ((ri[:, None] >= ri[None, :]) &
           (ri[:, None] // CHUNK == ri[None, :] // CHUNK)).astype(BF16)
    li = jnp.arange(GROUP_W) // RWKV_HEAD
    bd_mask = (li[:, None] == li[None, :]).astype(BF16)

    hx = x.reshape(batch * seq, D_MODEL)
    hm = jnp.concatenate([jnp.zeros((META_PAD, D_MODEL), x.dtype),
                          meta_tokens.astype(x.dtype)], axis=0)

    def row(p):
        return p.reshape(1, -1)

    l = 0
    wgu1, wd1 = ffn1_w_gate_up[l].astype(BF16), ffn1_w_down[l].astype(BF16)
    wgu2, wd2 = ffn2_w_gate_up[l].astype(BF16), ffn2_w_down[l].astype(BF16)
    w_in_l = w_in[l].astype(BF16)
    wa, wr, wo = (w_att_branch[l].astype(BF16), w_rwkv_branch[l].astype(BF16),
                  w_mix_out[l].astype(BF16))
    rw_params = (row(rwkv_mu[l]), row(rwkv_w0[l]), rwkv_w2[l], row(rwkv_a0[l]), rwkv_a2[l],
                 rwkv_g2[l], row(rwkv_k_k[l]), row(rwkv_k_a[l]), row(rwkv_r_k[l]),
                 row(rwkv_ln_w[l]), row(rwkv_ln_b[l]))

    outs = []
    for h, tabs in ((hm, tab_m), (hx, tab_x)):
        h1 = _ffn(h, row(ffn1_norm_pre[l]), wgu1, wd1, row(ffn1_norm_post[l]))
        outs.append((h1,) + tuple(_inproj(h1, row(mix_norm_pre[l]), w_in_l, *tabs)))
    (_, _, km, vm, zrm, _, _), (hx1, q, k, v, zr, ga, gr) = outs

    o_att = _attention(att_sinks[l], q, k, v, km, vm, batch, nblk)

    zero_state = jnp.zeros((N_GROUPS, GROUP_W, GROUP_W), F32)
    _, s_meta = _rwkv(zrm[None], jnp.zeros((1, RWKV_IN), F32), zero_state, tri, bd_mask,
                      rw_params)
    y_rwkv, _ = _rwkv(zr.reshape(batch, seq, RWKV_IN), zrm[BLOCK - 1:], s_meta, tri, bd_mask,
                      rw_params)
    y_rwkv = y_rwkv.reshape(batch * seq, RWKV_WIDTH)

    hx2 = _merge(hx1, o_att, y_rwkv, ga, gr, wa, wr, wo, row(mix_norm_post[l]))
    hx = _ffn(hx2, row(ffn2_norm_pre[l]), wgu2, wd2, row(ffn2_norm_post[l]))
    return hx.reshape(batch, seq, D_MODEL)
```

```python
import jax
import jax.numpy as jnp
import numpy as np
from jax import lax
from jax.experimental import pallas as pl
from jax.experimental.pallas import tpu as pltpu

D_MODEL = 1024
N_META = 16
ATT_HEADS = 8
ATT_KV_HEADS = 2
ATT_GROUP = ATT_HEADS // ATT_KV_HEADS
HEAD_DIM = 64
WINDOW = 128
BLOCK = 128
ROPE_THETA = 500000.0
ROT_DIM = HEAD_DIM // 4
ATT_Q = ATT_HEADS * HEAD_DIM
ATT_KV = ATT_KV_HEADS * HEAD_DIM
MASK_VALUE = -1e30
RWKV_HEADS = 8
RWKV_HEAD = 64
RWKV_WIDTH = RWKV_HEADS * RWKV_HEAD
DECAY_LORA = 64
AAA_LORA = 64
GATE_LORA = 128
GN_EPS = 64e-5
RWKV_IN = 3 * RWKV_WIDTH + DECAY_LORA + AAA_LORA + GATE_LORA
D_FF = 2816
NORM_EPS = 1e-6

META_PAD = BLOCK - N_META
CHUNK = 64
DECAY_SCALE = 0.6065306597126334
LOG2_E = 1.4426950408889634

VMEM_LIMIT = 56 * 1024 * 1024
ROW_TILE = 512

F32 = jnp.float32
BF16 = jnp.bfloat16


def _rms(x, gain):
    ms = jnp.mean(x * x, axis=-1, keepdims=True)
    return x * lax.rsqrt(ms + NORM_EPS) * gain


def _dot(a, b):
    return jnp.dot(a, b, preferred_element_type=F32)


def _const_spec(shape):
    nd = len(shape)
    return pl.BlockSpec(shape, lambda *_: (0,) * nd, pipeline_mode=pl.Buffered(1))


def _ffn_kernel(h_ref, gpre_ref, wgu_ref, wd_ref, gpost_ref, o_ref):
    h = h_ref[...]
    hn = _rms(h, gpre_ref[...]).astype(BF16)
    gate = _dot(hn, wgu_ref[:, :D_FF])
    up = _dot(hn, wgu_ref[:, D_FF:])
    act = (gate * jax.nn.sigmoid(gate) * up).astype(BF16)
    f = _dot(act, wd_ref[...])
    o_ref[...] = h + 0.5 * _rms(f, gpost_ref[...])


def _ffn(h, gpre, wgu, wd, gpost):
    rows = h.shape[0]
    tm = min(ROW_TILE, rows)
    return pl.pallas_call(
        _ffn_kernel,
        grid=(rows // tm,),
        in_specs=[
            pl.BlockSpec((tm, D_MODEL), lambda i: (i, 0)),
            _const_spec((1, D_MODEL)),
            _const_spec((D_MODEL, 2 * D_FF)),
            _const_spec((D_FF, D_MODEL)),
            _const_spec((1, D_MODEL)),
        ],
        out_specs=pl.BlockSpec((tm, D_MODEL), lambda i: (i, 0)),
        out_shape=jax.ShapeDtypeStruct((rows, D_MODEL), F32),
        compiler_params=pltpu.CompilerParams(
            dimension_semantics=("parallel",), vmem_limit_bytes=VMEM_LIMIT),
        name="ffn",
    )(h, gpre, wgu, wd, gpost)


C_Q = ATT_Q
C_K = C_Q + ATT_KV
C_V = C_K + ATT_KV
C_R = C_V + RWKV_IN
C_GA = C_R + D_MODEL
IN_COLS = C_GA + D_MODEL


def _rope(x, cos_t, sin_lo, sin_hi):
    n = x.shape[1] // 128
    if n > 1:
        cos_t = jnp.concatenate([cos_t] * n, axis=1)
        sin_lo = jnp.concatenate([sin_lo] * n, axis=1)
        sin_hi = jnp.concatenate([sin_hi] * n, axis=1)
    width = x.shape[1]
    half = ROT_DIM // 2
    from_hi = pltpu.roll(x, width - half, 1)
    from_lo = pltpu.roll(x, half, 1)
    return x * cos_t + from_hi * sin_lo + from_lo * sin_hi


def _inproj_kernel(h_ref, g_ref, w_ref, cos_ref, slo_ref, shi_ref,
                   q_ref, k_ref, v_ref, zr_ref, ga_ref, gr_ref):
    u = _rms(h_ref[...], g_ref[...]).astype(BF16)
    cos_t, sin_lo, sin_hi = cos_ref[...], slo_ref[...], shi_ref[...]
    q = _dot(u, w_ref[:, :C_Q])
    q_ref[...] = (_rope(q, cos_t, sin_lo, sin_hi) * (LOG2_E * HEAD_DIM ** -0.5)).astype(BF16)
    k = _dot(u, w_ref[:, C_Q:C_K])
    k_ref[...] = _rope(k, cos_t, sin_lo, sin_hi).astype(BF16)
    v_ref[...] = _dot(u, w_ref[:, C_K:C_V]).astype(BF16)
    zr_ref[...] = _dot(u, w_ref[:, C_V:C_R])
    ga_ref[...] = jax.nn.sigmoid(_dot(u, w_ref[:, C_R:C_GA])).astype(BF16)
    gr_ref[...] = jax.nn.sigmoid(_dot(u, w_ref[:, C_GA:])).astype(BF16)


def _inproj(h, gain, w_in, tables):
    rows = h.shape[0]
    tm = min(ROW_TILE, rows)
    tab_blocks = tables.shape[0] // tm

    def row(i):
        return (i, 0)

    def tab(c):
        return lambda i: (i % tab_blocks, c)

    widths = (ATT_Q, ATT_KV, ATT_KV, RWKV_IN, D_MODEL, D_MODEL)
    dtypes = (BF16, BF16, BF16, F32, BF16, BF16)
    return pl.pallas_call(
        _inproj_kernel,
        grid=(rows // tm,),
        in_specs=[
            pl.BlockSpec((tm, D_MODEL), row),
            _const_spec((1, D_MODEL)),
            _const_spec((D_MODEL, IN_COLS)),
            pl.BlockSpec((tm, 128), tab(0)),
            pl.BlockSpec((tm, 128), tab(1)),
            pl.BlockSpec((tm, 128), tab(2)),
        ],
        out_specs=[pl.BlockSpec((tm, w), row) for w in widths],
        out_shape=[jax.ShapeDtypeStruct((rows, w), d) for w, d in zip(widths, dtypes)],
        compiler_params=pltpu.CompilerParams(
            dimension_semantics=("parallel",), vmem_limit_bytes=VMEM_LIMIT),
        name="inproj",
    )(h, gain, w_in, tables, tables, tables)


def _rope_tables(pos):
    half = ROT_DIM // 2
    inv_freq = 1.0 / (ROPE_THETA ** (jnp.arange(half, dtype=F32) * (2.0 / ROT_DIM)))
    ang = pos[:, None] * inv_freq[None, :]
    trig = jnp.concatenate([jnp.cos(ang), jnp.sin(ang)], axis=1)
    lane = np.arange(128) % HEAD_DIM
    sel = np.zeros((2 * half, 3 * 128), np.float32)
    bias = np.zeros((1, 3 * 128), np.float32)
    for l in range(128):
        j = lane[l]
        if j < ROT_DIM:
            sel[j % half, l] = 1.0
        else:
            bias[0, l] = 1.0
        if j < half:
            sel[half + j, 128 + l] = -1.0
        elif j < ROT_DIM:
            sel[half + j - half, 256 + l] = 1.0
    return jnp.dot(trig, jnp.asarray(sel), precision=lax.Precision.HIGHEST) + jnp.asarray(bias)


ATT_TILE_BLOCKS = 4


def _attn_kernel(sink_ref, q_ref, k_ref, v_ref, kp_ref, vp_ref, km_ref, vm_ref, o_ref):
    first = pl.program_id(1) == 0
    k_prev = jnp.where(first, km_ref[...], kp_ref[...])
    v_prev = jnp.where(first, vm_ref[...], vp_ref[...])
    k_all = jnp.concatenate([k_prev, k_ref[...]], axis=0)
    v_all = jnp.concatenate([v_prev, v_ref[...]], axis=0)
    row = lax.broadcasted_iota(jnp.int32, (BLOCK, 2 * BLOCK), 0)
    col = lax.broadcasted_iota(jnp.int32, (BLOCK, 2 * BLOCK), 1)
    mask = (col > row) & (col <= row + WINDOW)
    mask_first = (col > jnp.where(first, jnp.maximum(row, META_PAD - 1), row)) & (col <= row + WINDOW)
    sink_col = col == 0
    sink_row = lax.broadcasted_iota(jnp.int32, (2 * BLOCK, ATT_KV), 0) == 0

    pairs = [(j, g) for j in range(ATT_TILE_BLOCKS) for g in range(ATT_KV_HEADS)]
    scores = {}
    for j, g in pairs:
        qj = q_ref[j * BLOCK:(j + 1) * BLOCK, :]
        kg = k_all[j * BLOCK:(j + 2) * BLOCK, g * HEAD_DIM:(g + 1) * HEAD_DIM]
        q4 = jnp.concatenate([qj[:, hd * HEAD_DIM:(hd + 1) * HEAD_DIM]
                              for hd in range(g * ATT_GROUP, (g + 1) * ATT_GROUP)], axis=0)
        scores[j, g] = lax.dot_general(q4, kg, (((1,), (1,)), ((), ())),
                                       preferred_element_type=F32)
    logits, maxima = {}, {}
    for j, g in pairs:
        mj = mask_first if j == 0 else mask
        for i in range(ATT_GROUP):
            sh = jnp.where(mj, scores[j, g][i * BLOCK:(i + 1) * BLOCK], MASK_VALUE)
            sh = jnp.where(sink_col, sink_ref[g * ATT_GROUP + i] * LOG2_E, sh)
            logits[j, g, i] = sh
            maxima[j, g, i] = jnp.max(sh, axis=-1, keepdims=True)
    probs, denoms = {}, {}
    for key, sh in logits.items():
        p = jnp.exp2(sh - maxima[key])
        denoms[key] = jnp.sum(p, axis=-1, keepdims=True)
        probs[key] = p.astype(BF16)
    for j in range(ATT_TILE_BLOCKS):
        vj = jnp.where(sink_row, 0.0, v_all[j * BLOCK:(j + 2) * BLOCK].astype(F32)).astype(BF16)
        outs = []
        for g in range(ATT_KV_HEADS):
            p4 = jnp.concatenate([probs[j, g, i] for i in range(ATT_GROUP)], axis=0)
            o4 = _dot(p4, vj[:, g * HEAD_DIM:(g + 1) * HEAD_DIM])
            outs += [o4[i * BLOCK:(i + 1) * BLOCK] / denoms[j, g, i] for i in range(ATT_GROUP)]
        o_ref[j * BLOCK:(j + 1) * BLOCK, :] = jnp.concatenate(outs, axis=1).astype(BF16)


def _attention(sinks, q, k, v, k_meta, v_meta, batch, nblk):
    tile = ATT_TILE_BLOCKS * BLOCK
    ntile = nblk // ATT_TILE_BLOCKS

    def own(b, n):
        return (b * ntile + n, 0)

    def prev(b, n):
        return (b * nblk + jnp.maximum(n * ATT_TILE_BLOCKS - 1, 0), 0)

    return pl.pallas_call(
        _attn_kernel,
        grid=(batch, ntile),
        in_specs=[
            pl.BlockSpec(memory_space=pltpu.SMEM),
            pl.BlockSpec((tile, ATT_Q), own),
            pl.BlockSpec((tile, ATT_KV), own),
            pl.BlockSpec((tile, ATT_KV), own),
            pl.BlockSpec((BLOCK, ATT_KV), prev),
            pl.BlockSpec((BLOCK, ATT_KV), prev),
            _const_spec((BLOCK, ATT_KV)),
            _const_spec((BLOCK, ATT_KV)),
        ],
        out_specs=pl.BlockSpec((tile, ATT_Q), own),
        out_shape=jax.ShapeDtypeStruct((batch * nblk * BLOCK, ATT_Q), BF16),
        compiler_params=pltpu.CompilerParams(
            dimension_semantics=("parallel", "parallel"), vmem_limit_bytes=VMEM_LIMIT),
        name="attention",
    )(sinks, q, k, v, k, v, k_meta, v_meta)


O_K = RWKV_WIDTH
O_V = 2 * RWKV_WIDTH
O_W = 3 * RWKV_WIDTH
O_A = O_W + DECAY_LORA
O_G = O_A + AAA_LORA

GROUP_HEADS = 4
GROUP_W = GROUP_HEADS * RWKV_HEAD
N_GROUPS = RWKV_HEADS // GROUP_HEADS
CUMSUM_PARTS = 2
INV_DOUBLINGS = 4


def _mm(a, b):
    return jnp.dot(a.astype(BF16), b.astype(BF16), preferred_element_type=F32)


def _mm_nt(a, b):
    return lax.dot_general(a.astype(BF16), b.astype(BF16), (((1,), (1,)), ((), ())),
                           preferred_element_type=F32)


def _mm_tn(a, b):
    return lax.dot_general(a, b, (((0,), (0,)), ((), ())), preferred_element_type=F32)


def _block_diag(x, bd_mask):
    xb = x.astype(BF16)
    reps = bd_mask.shape[0] // x.shape[0]
    return jnp.concatenate([xb] * reps, axis=0) * bd_mask


def _head_sum(x, bd_mask):
    return jnp.concatenate(
        [_mm(x[:, i * GROUP_W:(i + 1) * GROUP_W], bd_mask) for i in range(N_GROUPS)], axis=1)


def _chunk_cumsum(tri, x):
    pieces, rem = [], x
    for i in range(CUMSUM_PARTS):
        hi = rem.astype(BF16)
        pieces.append(hi)
        if i + 1 < CUMSUM_PARTS:
            rem = rem - hi.astype(F32)
    out = jnp.dot(tri, jnp.concatenate(pieces, axis=1), preferred_element_type=F32)
    w = x.shape[1]
    return sum(out[:, i * w:(i + 1) * w] for i in range(CUMSUM_PARTS))


def _rwkv_kernel(zr_ref, zprev0_ref, s0_ref, tri_ref, bd_ref, mu_ref, w0_ref, w2_ref,
                 a0_ref, a2_ref, g2_ref, kk_ref, ka_ref, rk_ref, lnw_ref, lnb_ref,
                 y_ref, sout_ref, state_scr, prev_scr, ybuf):
    nb = zr_ref.shape[0]
    n = pl.program_id(0)

    @pl.when(n == 0)
    def _():
        for b in range(nb):
            state_scr[b] = s0_ref[...]
            prev_scr[b] = zprev0_ref[...]

    row_id = lax.broadcasted_iota(jnp.int32, (BLOCK, RWKV_IN), 0)
    shifted = []
    for b in range(nb):
        z = zr_ref[b]
        z_prev = jnp.where(row_id == 0, prev_scr[b], pltpu.roll(z, 1, 0))
        prev_scr[b] = z[BLOCK - 1:BLOCK, :]
        shifted.append(z + (z_prev - z) * mu_ref[...])
    zs = jnp.concatenate(shifted, axis=0)

    r = zs[:, :O_K]
    k = zs[:, O_K:O_V]
    v = zs[:, O_V:O_W]
    xw = zs[:, O_W:O_A]
    xa = zs[:, O_A:O_G]
    xg = zs[:, O_G:]

    bd_mask = bd_ref[...]
    ld = -DECAY_SCALE * jax.nn.sigmoid(w0_ref[...] + _mm(jnp.tanh(xw), w2_ref[...]))
    a = jax.nn.sigmoid(a0_ref[...] + _mm(xa, a2_ref[...]))
    g = _mm(jax.nn.sigmoid(xg), g2_ref[...])
    kk = k * kk_ref[...]
    kk = kk / jnp.maximum(jnp.sqrt(_head_sum(kk * kk, bd_mask)), 1e-12)
    k = k * (1.0 + (a - 1.0) * ka_ref[...])

    tri = tri_ref[...]
    cum = jnp.concatenate(
        [_chunk_cumsum(tri, ld[b * BLOCK:(b + 1) * BLOCK]) for b in range(nb)], axis=0)
    e_pos = jnp.exp(cum)
    e_neg = jnp.exp(-cum)
    r_t = r * e_pos
    k_t = k * e_neg
    a_t = -kk * jnp.exp(cum - ld)
    b_t = kk * a * e_neg

    ti = lax.broadcasted_iota(jnp.int32, (CHUNK, GROUP_W), 0)
    si = lax.broadcasted_iota(jnp.int32, (CHUNK, GROUP_W), 1) % CHUNK
    strict = ti > si
    incl = ti >= si
    eye = (ti == si).astype(F32)

    n_chunks = BLOCK // CHUNK
    seqs = [(b, gi) for b in range(nb) for gi in range(N_GROUPS)]
    chains = [(b, c, gi) for c in range(n_chunks) for (b, gi) in seqs]

    def part(x, ch):
        b, c, gi = ch
        r0 = b * BLOCK + c * CHUNK
        return x[r0:r0 + CHUNK, gi * GROUP_W:(gi + 1) * GROUP_W]

    ar, m_ab, m_ak, m_rb, m_rk = {}, {}, {}, {}, {}
    for ch in chains:
        ar[ch] = jnp.concatenate([part(a_t, ch), part(r_t, ch)], axis=0)
        bd_bk = jnp.concatenate([_block_diag(part(b_t, ch), bd_mask),
                                 _block_diag(part(k_t, ch), bd_mask)], axis=0)
        sc = _mm_nt(ar[ch], bd_bk)
        m_ab[ch] = jnp.where(strict, sc[:CHUNK, :GROUP_W], 0.0)
        m_ak[ch] = jnp.where(strict, sc[:CHUNK, GROUP_W:], 0.0)
        m_rb[ch] = jnp.where(incl, sc[CHUNK:, :GROUP_W], 0.0)
        m_rk[ch] = jnp.where(incl, sc[CHUNK:, GROUP_W:], 0.0)

    inv = {ch: eye + m_ab[ch] for ch in chains}
    pw = {ch: _mm(m_ab[ch], _block_diag(m_ab[ch], bd_mask)) for ch in chains}
    for _ in range(INV_DOUBLINGS):
        for ch in chains:
            both = _mm(jnp.concatenate([pw[ch], inv[ch]], axis=0), _block_diag(pw[ch], bd_mask))
            pw[ch] = both[:CHUNK]
            inv[ch] = inv[ch] + both[CHUNK:]
    for ch in chains:
        inv[ch] = inv[ch] + _mm(inv[ch], _block_diag(pw[ch], bd_mask))

    mv = {ch: _mm(jnp.concatenate([m_ak[ch], m_rk[ch]], axis=0), _block_diag(part(v, ch), bd_mask))
          for ch in chains}

    state = {sq: state_scr[sq[0], sq[1]] for sq in seqs}
    bd_f32 = bd_mask.astype(F32)
    for c in range(n_chunks):
        cur = [(b, c, gi) for (b, gi) in seqs]
        xr0 = {ch: _mm_nt(ar[ch], state[(ch[0], ch[2])]) for ch in cur}
        u = {ch: _mm(inv[ch], _block_diag(xr0[ch][:CHUNK] + mv[ch][:CHUNK], bd_mask)) for ch in cur}
        for ch in cur:
            b, _, gi = ch
            y = xr0[ch][CHUNK:] + mv[ch][CHUNK:] + _mm(m_rb[ch], _block_diag(u[ch], bd_mask))
            upd = _mm_tn(jnp.concatenate([u[ch], part(v, ch)], axis=0),
                         jnp.concatenate([part(b_t, ch), part(k_t, ch)], axis=0))
            last = b * BLOCK + (c + 1) * CHUNK - 1
            w_end = e_pos[last:last + 1, gi * GROUP_W:(gi + 1) * GROUP_W]
            state[(b, gi)] = (state[(b, gi)] + upd * bd_f32) * w_end
            ybuf[b, c * CHUNK:(c + 1) * CHUNK, gi * GROUP_W:(gi + 1) * GROUP_W] = y
    for sq in seqs:
        state_scr[sq[0], sq[1]] = state[sq]
    for gi in range(N_GROUPS):
        sout_ref[gi] = state[(0, gi)]

    for b in range(nb):
        rows = slice(b * BLOCK, (b + 1) * BLOCK)
        y = ybuf[b]
        mean = _head_sum(y, bd_mask) * (1.0 / RWKV_HEAD)
        yc = y - mean
        var = _head_sum(yc * yc, bd_mask) * (1.0 / RWKV_HEAD)
        yn = yc * lax.rsqrt(var + GN_EPS) * lnw_ref[...] + lnb_ref[...]
        bonus = _head_sum(r[rows] * k[rows] * rk_ref[...], bd_mask) * v[rows]
        y_ref[b] = ((yn + bonus) * g[rows]).astype(BF16)


def _rwkv(zr, zprev0, s0, tri, bd_mask, params):
    batch, rows, _ = zr.shape
    state_shape = (N_GROUPS, GROUP_W, GROUP_W)
    return pl.pallas_call(
        _rwkv_kernel,
        grid=(rows // BLOCK,),
        in_specs=[
            pl.BlockSpec((batch, BLOCK, RWKV_IN), lambda n: (0, n, 0)),
            _const_spec((1, RWKV_IN)),
            _const_spec(state_shape),
            _const_spec((BLOCK, BLOCK)),
            _const_spec((GROUP_W, GROUP_W)),
        ] + [_const_spec(p.shape) for p in params],
        out_specs=[
            pl.BlockSpec((batch, BLOCK, RWKV_WIDTH), lambda n: (0, n, 0)),
            pl.BlockSpec(state_shape, lambda n: (0, 0, 0)),
        ],
        out_shape=[
            jax.ShapeDtypeStruct((batch, rows, RWKV_WIDTH), BF16),
            jax.ShapeDtypeStruct(state_shape, F32),
        ],
        scratch_shapes=[
            pltpu.VMEM((batch,) + state_shape, F32),
            pltpu.VMEM((batch, 1, RWKV_IN), F32),
            pltpu.VMEM((batch, BLOCK, RWKV_WIDTH), F32),
        ],
        compiler_params=pltpu.CompilerParams(
            dimension_semantics=("arbitrary",), vmem_limit_bytes=VMEM_LIMIT),
        name="rwkv",
    )(zr, zprev0, s0, tri, bd_mask, *params)


def _merge_kernel(h_ref, oa_ref, yr_ref, ga_ref, gr_ref, wa_ref, wr_ref, wo_ref, g_ref, o_ref):
    y_att = _dot(oa_ref[...], wa_ref[...])
    y_rwkv = _dot(yr_ref[...], wr_ref[...])
    merged = ga_ref[...].astype(F32) * y_att + gr_ref[...].astype(F32) * y_rwkv
    m = _dot(merged.astype(BF16), wo_ref[...])
    o_ref[...] = h_ref[...] + _rms(m, g_ref[...])


def _merge(h, o_att, y_rwkv, ga, gr, wa, wr, wo, gain):
    rows = h.shape[0]
    tm = min(ROW_TILE, rows)

    def row(i):
        return (i, 0)

    return pl.pallas_call(
        _merge_kernel,
        grid=(rows // tm,),
        in_specs=[
            pl.BlockSpec((tm, D_MODEL), row),
            pl.BlockSpec((tm, ATT_Q), row),
            pl.BlockSpec((tm, RWKV_WIDTH), row),
            pl.BlockSpec((tm, D_MODEL), row),
            pl.BlockSpec((tm, D_MODEL), row),
            _const_spec((ATT_Q, D_MODEL)),
            _const_spec((RWKV_WIDTH, D_MODEL)),
            _const_spec((D_MODEL, D_MODEL)),
            _const_spec((1, D_MODEL)),
        ],
        out_specs=pl.BlockSpec((tm, D_MODEL), row),
        out_shape=jax.ShapeDtypeStruct((rows, D_MODEL), F32),
        compiler_params=pltpu.CompilerParams(
            dimension_semantics=("parallel",), vmem_limit_bytes=VMEM_LIMIT),
        name="merge",
    )(h, o_att, y_rwkv, ga, gr, wa, wr, wo, gain)


def kernel(x, meta_tokens, ffn1_norm_pre, ffn1_w_gate_up, ffn1_w_down, ffn1_norm_post, mix_norm_pre, w_in, att_sinks, rwkv_mu, rwkv_w0, rwkv_w2, rwkv_a0, rwkv_a2, rwkv_g2, rwkv_k_k, rwkv_k_a, rwkv_r_k, rwkv_ln_w, rwkv_ln_b, w_att_branch, w_rwkv_branch, w_mix_out, mix_norm_post, ffn2_norm_pre, ffn2_w_gate_up, ffn2_w_down, ffn2_norm_post):
    batch, seq, _ = x.shape
    nblk = seq // BLOCK
    depth = w_in.shape[0]
    assert depth == 1 and seq % BLOCK == 0

    tab_x = _rope_tables(jnp.arange(N_META, N_META + seq, dtype=jnp.int32).astype(F32))
    pos_m = jnp.maximum(jnp.arange(BLOCK, dtype=jnp.int32) - META_PAD, 0).astype(F32)
    tab_m = _rope_tables(pos_m)
    ri = jnp.arange(BLOCK)
    tri = ((ri[:, None] >= ri[None, :]) &
           (ri[:, None] // CHUNK == ri[None, :] // CHUNK)).astype(BF16)
    li = jnp.arange(GROUP_W) // RWKV_HEAD
    bd_mask = (li[:, None] == li[None, :]).astype(BF16)

    hx = x.reshape(batch * seq, D_MODEL)
    hm = jnp.concatenate([jnp.zeros((META_PAD, D_MODEL), x.dtype),
                          meta_tokens.astype(x.dtype)], axis=0)

    def row(p):
        return p.reshape(1, -1)

    l = 0
    wgu1, wd1 = ffn1_w_gate_up[l].astype(BF16), ffn1_w_down[l].astype(BF16)
    wgu2, wd2 = ffn2_w_gate_up[l].astype(BF16), ffn2_w_down[l].astype(BF16)
    w_in_l = w_in[l].astype(BF16)
    wa, wr, wo = (w_att_branch[l].astype(BF16), w_rwkv_branch[l].astype(BF16),
                  w_mix_out[l].astype(BF16))
    rw_params = (row(rwkv_mu[l]), row(rwkv_w0[l]), rwkv_w2[l], row(rwkv_a0[l]), rwkv_a2[l],
                 rwkv_g2[l], row(rwkv_k_k[l]), row(rwkv_k_a[l]), row(rwkv_r_k[l]),
                 row(rwkv_ln_w[l]), row(rwkv_ln_b[l]))

    outs = []
    for h, tabs in ((hm, tab_m), (hx, tab_x)):
        h1 = _ffn(h, row(ffn1_norm_pre[l]), wgu1, wd1, row(ffn1_norm_post[l]))
        outs.append((h1,) + tuple(_inproj(h1, row(mix_norm_pre[l]), w_in_l, tabs)))
    (_, _, km, vm, zrm, _, _), (hx1, q, k, v, zr, ga, gr) = outs

    o_att = _attention(att_sinks[l], q, k, v, km, vm, batch, nblk)

    zero_state = jnp.zeros((N_GROUPS, GROUP_W, GROUP_W), F32)
    _, s_meta = _rwkv(zrm[None], jnp.zeros((1, RWKV_IN), F32), zero_state, tri, bd_mask,
                      rw_params)
    y_rwkv, _ = _rwkv(zr.reshape(batch, seq, RWKV_IN), zrm[BLOCK - 1:], s_meta, tri, bd_mask,
                      rw_params)
    y_rwkv = y_rwkv.reshape(batch * seq, RWKV_WIDTH)

    hx2 = _merge(hx1, o_att, y_rwkv, ga, gr, wa, wr, wo, row(mix_norm_post[l]))
    hx = _ffn(hx2, row(ffn2_norm_pre[l]), wgu2, wd2, row(ffn2_norm_post[l]))
    return hx.reshape(batch, seq, D_MODEL)
```

```python
import functools

import jax
import jax.numpy as jnp
import numpy as np
from jax import lax
from jax.experimental import pallas as pl
from jax.experimental.pallas import tpu as pltpu

D_MODEL = 1024
N_META = 16
ATT_HEADS = 8
ATT_KV_HEADS = 2
ATT_GROUP = ATT_HEADS // ATT_KV_HEADS
HEAD_DIM = 64
WINDOW = 128
BLOCK = 128
ROPE_THETA = 500000.0
ROT_DIM = HEAD_DIM // 4
ATT_Q = ATT_HEADS * HEAD_DIM
ATT_KV = ATT_KV_HEADS * HEAD_DIM
MASK_VALUE = -1e30
RWKV_HEADS = 8
RWKV_HEAD = 64
RWKV_WIDTH = RWKV_HEADS * RWKV_HEAD
DECAY_LORA = 64
AAA_LORA = 64
GATE_LORA = 128
GN_EPS = 64e-5
RWKV_IN = 3 * RWKV_WIDTH + DECAY_LORA + AAA_LORA + GATE_LORA
D_FF = 2816
NORM_EPS = 1e-6

META_PAD = BLOCK - N_META
CHUNK = 64
DECAY_SCALE = 0.6065306597126334
LOG2_E = 1.4426950408889634

VMEM_LIMIT = 56 * 1024 * 1024
ROW_TILE = 512
SUB_TILES = 2

F32 = jnp.float32
BF16 = jnp.bfloat16


def _rms(x, gain):
    ms = jnp.mean(x * x, axis=-1, keepdims=True)
    return x * lax.rsqrt(ms + NORM_EPS) * gain


def _dot(a, b):
    return jnp.dot(a, b, preferred_element_type=F32)


def _const_spec(shape):
    nd = len(shape)
    return pl.BlockSpec(shape, lambda *_: (0,) * nd, pipeline_mode=pl.Buffered(1))


def _sub_tiles(rows):
    sub = rows // SUB_TILES
    return [slice(i * sub, (i + 1) * sub) for i in range(SUB_TILES)]


def _ffn_kernel(*refs, with_merge):
    if with_merge:
        (h_ref, oa_ref, yr_ref, ga_ref, gr_ref, wa_ref, wr_ref, wo_ref, gmix_ref,
         gpre_ref, wgu_ref, wd_ref, gpost_ref, o_ref) = refs
    else:
        h_ref, gpre_ref, wgu_ref, wd_ref, gpost_ref, o_ref = refs
    parts = _sub_tiles(h_ref.shape[0])
    if with_merge:
        merged = []
        for p in parts:
            y_att = _dot(oa_ref[p, :], wa_ref[...])
            y_rwkv = _dot(yr_ref[p, :], wr_ref[...])
            merged.append((ga_ref[p, :].astype(F32) * y_att
                           + gr_ref[p, :].astype(F32) * y_rwkv).astype(BF16))
        hs = [h_ref[p, :] + _rms(_dot(m, wo_ref[...]), gmix_ref[...])
              for p, m in zip(parts, merged)]
    else:
        hs = [h_ref[p, :] for p in parts]
    acts = []
    for h in hs:
        hn = _rms(h, gpre_ref[...]).astype(BF16)
        gate = _dot(hn, wgu_ref[:, :D_FF])
        up = _dot(hn, wgu_ref[:, D_FF:])
        acts.append((gate * jax.nn.sigmoid(gate) * up).astype(BF16))
    for p, h, act in zip(parts, hs, acts):
        f = _dot(act, wd_ref[...])
        o_ref[p, :] = h + 0.5 * _rms(f, gpost_ref[...])


def _ffn(h, gpre, wgu, wd, gpost, merge=None):
    rows = h.shape[0]
    tm = min(ROW_TILE, rows)

    def row(i):
        return (i, 0)

    operands = [h]
    in_specs = [pl.BlockSpec((tm, D_MODEL), row)]
    if merge is not None:
        o_att, y_rwkv, ga, gr, wa, wr, wo, gmix = merge
        operands += [o_att, y_rwkv, ga, gr, wa, wr, wo, gmix]
        in_specs += [
            pl.BlockSpec((tm, ATT_Q), row),
            pl.BlockSpec((tm, RWKV_WIDTH), row),
            pl.BlockSpec((tm, D_MODEL), row),
            pl.BlockSpec((tm, D_MODEL), row),
            _const_spec((ATT_Q, D_MODEL)),
            _const_spec((RWKV_WIDTH, D_MODEL)),
            _const_spec((D_MODEL, D_MODEL)),
            _const_spec((1, D_MODEL)),
        ]
    operands += [gpre, wgu, wd, gpost]
    in_specs += [
        _const_spec((1, D_MODEL)),
        _const_spec((D_MODEL, 2 * D_FF)),
        _const_spec((D_FF, D_MODEL)),
        _const_spec((1, D_MODEL)),
    ]
    return pl.pallas_call(
        functools.partial(_ffn_kernel, with_merge=merge is not None),
        grid=(rows // tm,),
        in_specs=in_specs,
        out_specs=pl.BlockSpec((tm, D_MODEL), row),
        out_shape=jax.ShapeDtypeStruct((rows, D_MODEL), F32),
        compiler_params=pltpu.CompilerParams(
            dimension_semantics=("parallel",), vmem_limit_bytes=VMEM_LIMIT),
        name="merge_ffn" if merge is not None else "ffn",
    )(*operands)


C_Q = ATT_Q
C_K = C_Q + ATT_KV
C_V = C_K + ATT_KV
C_R = C_V + RWKV_IN
C_GA = C_R + D_MODEL
IN_COLS = C_GA + D_MODEL


def _rope(x, cos_t, sin_lo, sin_hi):
    n = x.shape[1] // 128
    if n > 1:
        cos_t = jnp.concatenate([cos_t] * n, axis=1)
        sin_lo = jnp.concatenate([sin_lo] * n, axis=1)
        sin_hi = jnp.concatenate([sin_hi] * n, axis=1)
    width = x.shape[1]
    half = ROT_DIM // 2
    from_hi = pltpu.roll(x, width - half, 1)
    from_lo = pltpu.roll(x, half, 1)
    return x * cos_t + from_hi * sin_lo + from_lo * sin_hi


def _inproj_kernel(h_ref, g_ref, w_ref, cos_ref, slo_ref, shi_ref,
                   q_ref, k_ref, v_ref, zr_ref, ga_ref, gr_ref):
    for p in _sub_tiles(h_ref.shape[0]):
        u = _rms(h_ref[p, :], g_ref[...]).astype(BF16)
        cos_t, sin_lo, sin_hi = cos_ref[p, :], slo_ref[p, :], shi_ref[p, :]
        q = _dot(u, w_ref[:, :C_Q])
        q_ref[p, :] = (_rope(q, cos_t, sin_lo, sin_hi) * (LOG2_E * HEAD_DIM ** -0.5)).astype(BF16)
        k = _dot(u, w_ref[:, C_Q:C_K])
        k_ref[p, :] = _rope(k, cos_t, sin_lo, sin_hi).astype(BF16)
        v_ref[p, :] = _dot(u, w_ref[:, C_K:C_V]).astype(BF16)
        zr_ref[p, :] = _dot(u, w_ref[:, C_V:C_R])
        ga_ref[p, :] = jax.nn.sigmoid(_dot(u, w_ref[:, C_R:C_GA])).astype(BF16)
        gr_ref[p, :] = jax.nn.sigmoid(_dot(u, w_ref[:, C_GA:])).astype(BF16)


def _inproj(h, gain, w_in, tables):
    rows = h.shape[0]
    tm = min(ROW_TILE, rows)
    tab_blocks = tables.shape[0] // tm

    def row(i):
        return (i, 0)

    def tab(c):
        return lambda i: (i % tab_blocks, c)

    widths = (ATT_Q, ATT_KV, ATT_KV, RWKV_IN, D_MODEL, D_MODEL)
    dtypes = (BF16, BF16, BF16, F32, BF16, BF16)
    return pl.pallas_call(
        _inproj_kernel,
        grid=(rows // tm,),
        in_specs=[
            pl.BlockSpec((tm, D_MODEL), row),
            _const_spec((1, D_MODEL)),
            _const_spec((D_MODEL, IN_COLS)),
            pl.BlockSpec((tm, 128), tab(0)),
            pl.BlockSpec((tm, 128), tab(1)),
            pl.BlockSpec((tm, 128), tab(2)),
        ],
        out_specs=[pl.BlockSpec((tm, w), row) for w in widths],
        out_shape=[jax.ShapeDtypeStruct((rows, w), d) for w, d in zip(widths, dtypes)],
        compiler_params=pltpu.CompilerParams(
            dimension_semantics=("parallel",), vmem_limit_bytes=VMEM_LIMIT),
        name="inproj",
    )(h, gain, w_in, tables, tables, tables)


def _rope_tables(pos):
    half = ROT_DIM // 2
    inv_freq = 1.0 / (ROPE_THETA ** (jnp.arange(half, dtype=F32) * (2.0 / ROT_DIM)))
    ang = pos[:, None] * inv_freq[None, :]
    trig = jnp.concatenate([jnp.cos(ang), jnp.sin(ang)], axis=1)
    lane = np.arange(128) % HEAD_DIM
    sel = np.zeros((2 * half, 3 * 128), np.float32)
    bias = np.zeros((1, 3 * 128), np.float32)
    for l in range(128):
        j = lane[l]
        if j < ROT_DIM:
            sel[j % half, l] = 1.0
        else:
            bias[0, l] = 1.0
        if j < half:
            sel[half + j, 128 + l] = -1.0
        elif j < ROT_DIM:
            sel[half + j - half, 256 + l] = 1.0
    return jnp.dot(trig, jnp.asarray(sel), precision=lax.Precision.HIGHEST) + jnp.asarray(bias)


ATT_TILE_BLOCKS = 4


def _attn_kernel(sink_ref, q_ref, k_ref, v_ref, kp_ref, vp_ref, km_ref, vm_ref, o_ref):
    first = pl.program_id(1) == 0
    k_prev = jnp.where(first, km_ref[...], kp_ref[...])
    v_prev = jnp.where(first, vm_ref[...], vp_ref[...])
    k_all = jnp.concatenate([k_prev, k_ref[...]], axis=0)
    v_all = jnp.concatenate([v_prev, v_ref[...]], axis=0)
    row = lax.broadcasted_iota(jnp.int32, (BLOCK, 2 * BLOCK), 0)
    col = lax.broadcasted_iota(jnp.int32, (BLOCK, 2 * BLOCK), 1)
    mask = (col > row) & (col <= row + WINDOW)
    mask_first = (col > jnp.where(first, jnp.maximum(row, META_PAD - 1), row)) & (col <= row + WINDOW)
    sink_col = col == 0
    sink_row = lax.broadcasted_iota(jnp.int32, (2 * BLOCK, ATT_KV), 0) == 0

    pairs = [(j, g) for j in range(ATT_TILE_BLOCKS) for g in range(ATT_KV_HEADS)]
    scores = {}
    for j, g in pairs:
        qj = q_ref[j * BLOCK:(j + 1) * BLOCK, :]
        kg = k_all[j * BLOCK:(j + 2) * BLOCK, g * HEAD_DIM:(g + 1) * HEAD_DIM]
        q4 = jnp.concatenate([qj[:, hd * HEAD_DIM:(hd + 1) * HEAD_DIM]
                              for hd in range(g * ATT_GROUP, (g + 1) * ATT_GROUP)], axis=0)
        scores[j, g] = lax.dot_general(q4, kg, (((1,), (1,)), ((), ())),
                                       preferred_element_type=F32)
    logits, maxima = {}, {}
    for j, g in pairs:
        mj = mask_first if j == 0 else mask
        for i in range(ATT_GROUP):
            sh = jnp.where(mj, scores[j, g][i * BLOCK:(i + 1) * BLOCK], MASK_VALUE)
            sh = jnp.where(sink_col, sink_ref[g * ATT_GROUP + i] * LOG2_E, sh)
            logits[j, g, i] = sh
            maxima[j, g, i] = jnp.max(sh, axis=-1, keepdims=True)
    probs, denoms = {}, {}
    for key, sh in logits.items():
        p = jnp.exp2(sh - maxima[key])
        denoms[key] = jnp.sum(p, axis=-1, keepdims=True)
        probs[key] = p.astype(BF16)
    for j in range(ATT_TILE_BLOCKS):
        vj = jnp.where(sink_row, 0.0, v_all[j * BLOCK:(j + 2) * BLOCK].astype(F32)).astype(BF16)
        outs = []
        for g in range(ATT_KV_HEADS):
            p4 = jnp.concatenate([probs[j, g, i] for i in range(ATT_GROUP)], axis=0)
            o4 = _dot(p4, vj[:, g * HEAD_DIM:(g + 1) * HEAD_DIM])
            outs += [o4[i * BLOCK:(i + 1) * BLOCK] / denoms[j, g, i] for i in range(ATT_GROUP)]
        o_ref[j * BLOCK:(j + 1) * BLOCK, :] = jnp.concatenate(outs, axis=1).astype(BF16)


def _attention(sinks, q, k, v, k_meta, v_meta, batch, nblk):
    tile = ATT_TILE_BLOCKS * BLOCK
    ntile = nblk // ATT_TILE_BLOCKS

    def own(b, n):
        return (b * ntile + n, 0)

    def prev(b, n):
        return (b * nblk + jnp.maximum(n * ATT_TILE_BLOCKS - 1, 0), 0)

    return pl.pallas_call(
        _attn_kernel,
        grid=(batch, ntile),
        in_specs=[
            pl.BlockSpec(memory_space=pltpu.SMEM),
            pl.BlockSpec((tile, ATT_Q), own),
            pl.BlockSpec((tile, ATT_KV), own),
            pl.BlockSpec((tile, ATT_KV), own),
            pl.BlockSpec((BLOCK, ATT_KV), prev),
            pl.BlockSpec((BLOCK, ATT_KV), prev),
            _const_spec((BLOCK, ATT_KV)),
            _const_spec((BLOCK, ATT_KV)),
        ],
        out_specs=pl.BlockSpec((tile, ATT_Q), own),
        out_shape=jax.ShapeDtypeStruct((batch * nblk * BLOCK, ATT_Q), BF16),
        compiler_params=pltpu.CompilerParams(
            dimension_semantics=("parallel", "parallel"), vmem_limit_bytes=VMEM_LIMIT),
        name="attention",
    )(sinks, q, k, v, k, v, k_meta, v_meta)


O_K = RWKV_WIDTH
O_V = 2 * RWKV_WIDTH
O_W = 3 * RWKV_WIDTH
O_A = O_W + DECAY_LORA
O_G = O_A + AAA_LORA

GROUP_HEADS = 4
GROUP_W = GROUP_HEADS * RWKV_HEAD
N_GROUPS = RWKV_HEADS // GROUP_HEADS
CUMSUM_PARTS = 2
INV_DOUBLINGS = 4


def _mm(a, b):
    return jnp.dot(a.astype(BF16), b.astype(BF16), preferred_element_type=F32)


def _mm_nt(a, b):
    return lax.dot_general(a.astype(BF16), b.astype(BF16), (((1,), (1,)), ((), ())),
                           preferred_element_type=F32)


def _mm_tn(a, b):
    return lax.dot_general(a, b, (((0,), (0,)), ((), ())), preferred_element_type=F32)


def _block_diag(x, bd_mask):
    xb = x.astype(BF16)
    reps = bd_mask.shape[0] // x.shape[0]
    return jnp.concatenate([xb] * reps, axis=0) * bd_mask


def _head_sum(x, bd_mask):
    return jnp.concatenate(
        [_mm(x[:, i * GROUP_W:(i + 1) * GROUP_W], bd_mask) for i in range(N_GROUPS)], axis=1)


def _chunk_cumsum(tri, x):
    pieces, rem = [], x
    for i in range(CUMSUM_PARTS):
        hi = rem.astype(BF16)
        pieces.append(hi)
        if i + 1 < CUMSUM_PARTS:
            rem = rem - hi.astype(F32)
    out = jnp.dot(tri, jnp.concatenate(pieces, axis=1), preferred_element_type=F32)
    w = x.shape[1]
    return sum(out[:, i * w:(i + 1) * w] for i in range(CUMSUM_PARTS))


def _rwkv_kernel(zr_ref, zprev0_ref, s0_ref, tri_ref, bd_ref, mu_ref, w0_ref, w2_ref,
                 a0_ref, a2_ref, g2_ref, kk_ref, ka_ref, rk_ref, lnw_ref, lnb_ref,
                 y_ref, sout_ref, state_scr, prev_scr, ybuf):
    nb = zr_ref.shape[0]
    n = pl.program_id(0)

    @pl.when(n == 0)
    def _():
        for b in range(nb):
            state_scr[b] = s0_ref[...]
            prev_scr[b] = zprev0_ref[...]

    row_id = lax.broadcasted_iota(jnp.int32, (BLOCK, RWKV_IN), 0)
    shifted = []
    for b in range(nb):
        z = zr_ref[b]
        z_prev = jnp.where(row_id == 0, prev_scr[b], pltpu.roll(z, 1, 0))
        prev_scr[b] = z[BLOCK - 1:BLOCK, :]
        shifted.append(z + (z_prev - z) * mu_ref[...])
    zs = jnp.concatenate(shifted, axis=0)

    r = zs[:, :O_K]
    k = zs[:, O_K:O_V]
    v = zs[:, O_V:O_W]
    xw = zs[:, O_W:O_A]
    xa = zs[:, O_A:O_G]
    xg = zs[:, O_G:]

    bd_mask = bd_ref[...]
    ld = -DECAY_SCALE * jax.nn.sigmoid(w0_ref[...] + _mm(jnp.tanh(xw), w2_ref[...]))
    a = jax.nn.sigmoid(a0_ref[...] + _mm(xa, a2_ref[...]))
    g = _mm(jax.nn.sigmoid(xg), g2_ref[...])
    kk = k * kk_ref[...]
    kk = kk / jnp.maximum(jnp.sqrt(_head_sum(kk * kk, bd_mask)), 1e-12)
    k = k * (1.0 + (a - 1.0) * ka_ref[...])

    tri = tri_ref[...]
    cum = jnp.concatenate(
        [_chunk_cumsum(tri, ld[b * BLOCK:(b + 1) * BLOCK]) for b in range(nb)], axis=0)
    e_pos = jnp.exp(cum)
    e_neg = jnp.exp(-cum)
    r_t = r * e_pos
    k_t = k * e_neg
    a_t = -kk * jnp.exp(cum - ld)
    b_t = kk * a * e_neg

    ti = lax.broadcasted_iota(jnp.int32, (CHUNK, GROUP_W), 0)
    si = lax.broadcasted_iota(jnp.int32, (CHUNK, GROUP_W), 1) % CHUNK
    strict = ti > si
    incl = ti >= si
    eye = (ti == si).astype(F32)

    n_chunks = BLOCK // CHUNK
    seqs = [(b, gi) for b in range(nb) for gi in range(N_GROUPS)]
    chains = [(b, c, gi) for c in range(n_chunks) for (b, gi) in seqs]

    def part(x, ch):
        b, c, gi = ch
        r0 = b * BLOCK + c * CHUNK
        return x[r0:r0 + CHUNK, gi * GROUP_W:(gi + 1) * GROUP_W]

    ar, m_ab, m_ak, m_rb, m_rk = {}, {}, {}, {}, {}
    for ch in chains:
        ar[ch] = jnp.concatenate([part(a_t, ch), part(r_t, ch)], axis=0)
        bd_bk = jnp.concatenate([_block_diag(part(b_t, ch), bd_mask),
                                 _block_diag(part(k_t, ch), bd_mask)], axis=0)
        sc = _mm_nt(ar[ch], bd_bk)
        m_ab[ch] = jnp.where(strict, sc[:CHUNK, :GROUP_W], 0.0)
        m_ak[ch] = jnp.where(strict, sc[:CHUNK, GROUP_W:], 0.0)
        m_rb[ch] = jnp.where(incl, sc[CHUNK:, :GROUP_W], 0.0)
        m_rk[ch] = jnp.where(incl, sc[CHUNK:, GROUP_W:], 0.0)

    inv = {ch: eye + m_ab[ch] for ch in chains}
    pw = {ch: _mm(m_ab[ch], _block_diag(m_ab[ch], bd_mask)) for ch in chains}
    for _ in range(INV_DOUBLINGS):
        for ch in chains:
            both = _mm(jnp.concatenate([pw[ch], inv[ch]], axis=0), _block_diag(pw[ch], bd_mask))
            pw[ch] = both[:CHUNK]
            inv[ch] = inv[ch] + both[CHUNK:]
    for ch in chains:
        inv[ch] = inv[ch] + _mm(inv[ch], _block_diag(pw[ch], bd_mask))

    mv = {ch: _mm(jnp.concatenate([m_ak[ch], m_rk[ch]], axis=0), _block_diag(part(v, ch), bd_mask))
          for ch in chains}

    state = {sq: state_scr[sq[0], sq[1]] for sq in seqs}
    bd_f32 = bd_mask.astype(F32)
    for c in range(n_chunks):
        cur = [(b, c, gi) for (b, gi) in seqs]
        xr0 = {ch: _mm_nt(ar[ch], state[(ch[0], ch[2])]) for ch in cur}
        u = {ch: _mm(inv[ch], _block_diag(xr0[ch][:CHUNK] + mv[ch][:CHUNK], bd_mask)) for ch in cur}
        for ch in cur:
            b, _, gi = ch
            y = xr0[ch][CHUNK:] + mv[ch][CHUNK:] + _mm(m_rb[ch], _block_diag(u[ch], bd_mask))
            upd = _mm_tn(jnp.concatenate([u[ch], part(v, ch)], axis=0),
                         jnp.concatenate([part(b_t, ch), part(k_t, ch)], axis=0))
            last = b * BLOCK + (c + 1) * CHUNK - 1
            w_end = e_pos[last:last + 1, gi * GROUP_W:(gi + 1) * GROUP_W]
            state[(b, gi)] = (state[(b, gi)] + upd * bd_f32) * w_end
            ybuf[b, c * CHUNK:(c + 1) * CHUNK, gi * GROUP_W:(gi + 1) * GROUP_W] = y
    for sq in seqs:
        state_scr[sq[0], sq[1]] = state[sq]
    for gi in range(N_GROUPS):
        sout_ref[gi] = state[(0, gi)]

    for b in range(nb):
        rows = slice(b * BLOCK, (b + 1) * BLOCK)
        y = ybuf[b]
        mean = _head_sum(y, bd_mask) * (1.0 / RWKV_HEAD)
        yc = y - mean
        var = _head_sum(yc * yc, bd_mask) * (1.0 / RWKV_HEAD)
        yn = yc * lax.rsqrt(var + GN_EPS) * lnw_ref[...] + lnb_ref[...]
        bonus = _head_sum(r[rows] * k[rows] * rk_ref[...], bd_mask) * v[rows]
        y_ref[b] = ((yn + bonus) * g[rows]).astype(BF16)


def _rwkv(zr, zprev0, s0, tri, bd_mask, params):
    batch, rows, _ = zr.shape
    state_shape = (N_GROUPS, GROUP_W, GROUP_W)
    return pl.pallas_call(
        _rwkv_kernel,
        grid=(rows // BLOCK,),
        in_specs=[
            pl.BlockSpec((batch, BLOCK, RWKV_IN), lambda n: (0, n, 0)),
            _const_spec((1, RWKV_IN)),
            _const_spec(state_shape),
            _const_spec((BLOCK, BLOCK)),
            _const_spec((GROUP_W, GROUP_W)),
        ] + [_const_spec(p.shape) for p in params],
        out_specs=[
            pl.BlockSpec((batch, BLOCK, RWKV_WIDTH), lambda n: (0, n, 0)),
            pl.BlockSpec(state_shape, lambda n: (0, 0, 0)),
        ],
        out_shape=[
            jax.ShapeDtypeStruct((batch, rows, RWKV_WIDTH), BF16),
            jax.ShapeDtypeStruct(state_shape, F32),
        ],
        scratch_shapes=[
            pltpu.VMEM((batch,) + state_shape, F32),
            pltpu.VMEM((batch, 1, RWKV_IN), F32),
            pltpu.VMEM((batch, BLOCK, RWKV_WIDTH), F32),
        ],
        compiler_params=pltpu.CompilerParams(
            dimension_semantics=("arbitrary",), vmem_limit_bytes=VMEM_LIMIT),
        name="rwkv",
    )(zr, zprev0, s0, tri, bd_mask, *params)


def kernel(x, meta_tokens, ffn1_norm_pre, ffn1_w_gate_up, ffn1_w_down, ffn1_norm_post, mix_norm_pre, w_in, att_sinks, rwkv_mu, rwkv_w0, rwkv_w2, rwkv_a0, rwkv_a2, rwkv_g2, rwkv_k_k, rwkv_k_a, rwkv_r_k, rwkv_ln_w, rwkv_ln_b, w_att_branch, w_rwkv_branch, w_mix_out, mix_norm_post, ffn2_norm_pre, ffn2_w_gate_up, ffn2_w_down, ffn2_norm_post):
    batch, seq, _ = x.shape
    nblk = seq // BLOCK
    depth = w_in.shape[0]
    assert depth == 1 and seq % BLOCK == 0

    tab_x = _rope_tables(jnp.arange(N_META, N_META + seq, dtype=jnp.int32).astype(F32))
    pos_m = jnp.maximum(jnp.arange(BLOCK, dtype=jnp.int32) - META_PAD, 0).astype(F32)
    tab_m = _rope_tables(pos_m)
    ri = jnp.arange(BLOCK)
    tri = ((ri[:, None] >= ri[None, :]) &
           (ri[:, None] // CHUNK == ri[None, :] // CHUNK)).astype(BF16)
    li = jnp.arange(GROUP_W) // RWKV_HEAD
    bd_mask = (li[:, None] == li[None, :]).astype(BF16)

    hx = x.reshape(batch * seq, D_MODEL)
    hm = jnp.concatenate([jnp.zeros((META_PAD, D_MODEL), x.dtype),
                          meta_tokens.astype(x.dtype)], axis=0)

    def row(p):
        return p.reshape(1, -1)

    l = 0
    wgu1, wd1 = ffn1_w_gate_up[l].astype(BF16), ffn1_w_down[l].astype(BF16)
    wgu2, wd2 = ffn2_w_gate_up[l].astype(BF16), ffn2_w_down[l].astype(BF16)
    w_in_l = w_in[l].astype(BF16)
    wa, wr, wo = (w_att_branch[l].astype(BF16), w_rwkv_branch[l].astype(BF16),
                  w_mix_out[l].astype(BF16))
    rw_params = (row(rwkv_mu[l]), row(rwkv_w0[l]), rwkv_w2[l], row(rwkv_a0[l]), rwkv_a2[l],
                 rwkv_g2[l], row(rwkv_k_k[l]), row(rwkv_k_a[l]), row(rwkv_r_k[l]),
                 row(rwkv_ln_w[l]), row(rwkv_ln_b[l]))

    outs = []
    for h, tabs in ((hm, tab_m), (hx, tab_x)):
        h1 = _ffn(h, row(ffn1_norm_pre[l]), wgu1, wd1, row(ffn1_norm_post[l]))
        outs.append((h1,) + tuple(_inproj(h1, row(mix_norm_pre[l]), w_in_l, tabs)))
    (_, _, km, vm, zrm, _, _), (hx1, q, k, v, zr, ga, gr) = outs

    o_att = _attention(att_sinks[l], q, k, v, km, vm, batch, nblk)

    zero_state = jnp.zeros((N_GROUPS, GROUP_W, GROUP_W), F32)
    _, s_meta = _rwkv(zrm[None], jnp.zeros((1, RWKV_IN), F32), zero_state, tri, bd_mask,
                      rw_params)
    y_rwkv, _ = _rwkv(zr.reshape(batch, seq, RWKV_IN), zrm[BLOCK - 1:], s_meta, tri, bd_mask,
                      rw_params)
    y_rwkv = y_rwkv.reshape(batch * seq, RWKV_WIDTH)

    hx = _ffn(hx1, row(ffn2_norm_pre[l]), wgu2, wd2, row(ffn2_norm_post[l]),
              merge=(o_att, y_rwkv, ga, gr, wa, wr, wo, row(mix_norm_post[l])))
    return hx.reshape(batch, seq, D_MODEL)
```

```python
import functools

import jax
import jax.numpy as jnp
import numpy as np
from jax import lax
from jax.experimental import pallas as pl
from jax.experimental.pallas import tpu as pltpu

D_MODEL = 1024
N_META = 16
ATT_HEADS = 8
ATT_KV_HEADS = 2
ATT_GROUP = ATT_HEADS // ATT_KV_HEADS
HEAD_DIM = 64
WINDOW = 128
BLOCK = 128
ROPE_THETA = 500000.0
ROT_DIM = HEAD_DIM // 4
ATT_Q = ATT_HEADS * HEAD_DIM
ATT_KV = ATT_KV_HEADS * HEAD_DIM
MASK_VALUE = -1e30
RWKV_HEADS = 8
RWKV_HEAD = 64
RWKV_WIDTH = RWKV_HEADS * RWKV_HEAD
DECAY_LORA = 64
AAA_LORA = 64
GATE_LORA = 128
GN_EPS = 64e-5
RWKV_IN = 3 * RWKV_WIDTH + DECAY_LORA + AAA_LORA + GATE_LORA
D_FF = 2816
NORM_EPS = 1e-6

META_PAD = BLOCK - N_META
CHUNK = 64
DECAY_SCALE = 0.6065306597126334
LOG2_E = 1.4426950408889634

VMEM_LIMIT = 56 * 1024 * 1024
ROW_TILE = 512
SUB_TILES = 2

F32 = jnp.float32
BF16 = jnp.bfloat16


def _rms(x, gain):
    ms = jnp.mean(x * x, axis=-1, keepdims=True)
    return x * lax.rsqrt(ms + NORM_EPS) * gain


def _dot(a, b):
    return jnp.dot(a, b, preferred_element_type=F32)


def _const_spec(shape):
    nd = len(shape)
    return pl.BlockSpec(shape, lambda *_: (0,) * nd, pipeline_mode=pl.Buffered(1))


def _sub_tiles(rows):
    sub = rows // SUB_TILES
    return [slice(i * sub, (i + 1) * sub) for i in range(SUB_TILES)]


def _ffn_kernel(*refs, with_merge):
    if with_merge:
        (h_ref, oa_ref, yr_ref, ga_ref, gr_ref, wa_ref, wr_ref, wo_ref, gmix_ref,
         gpre_ref, wgu_ref, wd_ref, gpost_ref, o_ref) = refs
    else:
        h_ref, gpre_ref, wgu_ref, wd_ref, gpost_ref, o_ref = refs
    parts = _sub_tiles(h_ref.shape[0])
    if with_merge:
        merged = []
        for p in parts:
            y_att = _dot(oa_ref[p, :], wa_ref[...])
            y_rwkv = _dot(yr_ref[p, :], wr_ref[...])
            merged.append((ga_ref[p, :].astype(F32) * y_att
                           + gr_ref[p, :].astype(F32) * y_rwkv).astype(BF16))
        hs = [h_ref[p, :] + _rms(_dot(m, wo_ref[...]), gmix_ref[...])
              for p, m in zip(parts, merged)]
    else:
        hs = [h_ref[p, :] for p in parts]
    acts = []
    for h in hs:
        hn = _rms(h, gpre_ref[...]).astype(BF16)
        gate = _dot(hn, wgu_ref[:, :D_FF])
        up = _dot(hn, wgu_ref[:, D_FF:])
        acts.append((gate * jax.nn.sigmoid(gate) * up).astype(BF16))
    for p, h, act in zip(parts, hs, acts):
        f = _dot(act, wd_ref[...])
        o_ref[p, :] = h + 0.5 * _rms(f, gpost_ref[...])


def _ffn(h, gpre, wgu, wd, gpost, merge=None):
    rows = h.shape[0]
    tm = min(ROW_TILE, rows)

    def row(i):
        return (i, 0)

    operands = [h]
    in_specs = [pl.BlockSpec((tm, D_MODEL), row)]
    if merge is not None:
        o_att, y_rwkv, ga, gr, wa, wr, wo, gmix = merge
        operands += [o_att, y_rwkv, ga, gr, wa, wr, wo, gmix]
        in_specs += [
            pl.BlockSpec((tm, ATT_Q), row),
            pl.BlockSpec((tm, RWKV_WIDTH), row),
            pl.BlockSpec((tm, D_MODEL), row),
            pl.BlockSpec((tm, D_MODEL), row),
            _const_spec((ATT_Q, D_MODEL)),
            _const_spec((RWKV_WIDTH, D_MODEL)),
            _const_spec((D_MODEL, D_MODEL)),
            _const_spec((1, D_MODEL)),
        ]
    operands += [gpre, wgu, wd, gpost]
    in_specs += [
        _const_spec((1, D_MODEL)),
        _const_spec((D_MODEL, 2 * D_FF)),
        _const_spec((D_FF, D_MODEL)),
        _const_spec((1, D_MODEL)),
    ]
    return pl.pallas_call(
        functools.partial(_ffn_kernel, with_merge=merge is not None),
        grid=(rows // tm,),
        in_specs=in_specs,
        out_specs=pl.BlockSpec((tm, D_MODEL), row),
        out_shape=jax.ShapeDtypeStruct((rows, D_MODEL), F32),
        compiler_params=pltpu.CompilerParams(
            dimension_semantics=("parallel",), vmem_limit_bytes=VMEM_LIMIT),
        name="merge_ffn" if merge is not None else "ffn",
    )(*operands)


C_Q = ATT_Q
C_K = C_Q + ATT_KV
C_V = C_K + ATT_KV
C_R = C_V + RWKV_IN
C_GA = C_R + D_MODEL
IN_COLS = C_GA + D_MODEL


def _rope(x, cos_t, sin_lo, sin_hi):
    n = x.shape[1] // 128
    if n > 1:
        cos_t = jnp.concatenate([cos_t] * n, axis=1)
        sin_lo = jnp.concatenate([sin_lo] * n, axis=1)
        sin_hi = jnp.concatenate([sin_hi] * n, axis=1)
    width = x.shape[1]
    half = ROT_DIM // 2
    from_hi = pltpu.roll(x, width - half, 1)
    from_lo = pltpu.roll(x, half, 1)
    return x * cos_t + from_hi * sin_lo + from_lo * sin_hi


def _inproj_kernel(h_ref, g_ref, w_ref, cos_ref, slo_ref, shi_ref,
                   q_ref, k_ref, v_ref, zr_ref, ga_ref, gr_ref):
    for p in _sub_tiles(h_ref.shape[0]):
        u = _rms(h_ref[p, :], g_ref[...]).astype(BF16)
        cos_t, sin_lo, sin_hi = cos_ref[p, :], slo_ref[p, :], shi_ref[p, :]
        q = _dot(u, w_ref[:, :C_Q])
        q_ref[p, :] = (_rope(q, cos_t, sin_lo, sin_hi) * (LOG2_E * HEAD_DIM ** -0.5)).astype(BF16)
        k = _dot(u, w_ref[:, C_Q:C_K])
        k_ref[p, :] = _rope(k, cos_t, sin_lo, sin_hi).astype(BF16)
        v_ref[p, :] = _dot(u, w_ref[:, C_K:C_V]).astype(BF16)
        zr_ref[p, :] = _dot(u, w_ref[:, C_V:C_R])
        ga_ref[p, :] = jax.nn.sigmoid(_dot(u, w_ref[:, C_R:C_GA])).astype(BF16)
        gr_ref[p, :] = jax.nn.sigmoid(_dot(u, w_ref[:, C_GA:])).astype(BF16)


def _inproj(h, gain, w_in, tables):
    rows = h.shape[0]
    tm = min(ROW_TILE, rows)
    tab_blocks = tables.shape[0] // tm

    def row(i):
        return (i, 0)

    def tab(c):
        return lambda i: (i % tab_blocks, c)

    widths = (ATT_Q, ATT_KV, ATT_KV, RWKV_IN, D_MODEL, D_MODEL)
    dtypes = (BF16, BF16, BF16, F32, BF16, BF16)
    return pl.pallas_call(
        _inproj_kernel,
        grid=(rows // tm,),
        in_specs=[
            pl.BlockSpec((tm, D_MODEL), row),
            _const_spec((1, D_MODEL)),
            _const_spec((D_MODEL, IN_COLS)),
            pl.BlockSpec((tm, 128), tab(0)),
            pl.BlockSpec((tm, 128), tab(1)),
            pl.BlockSpec((tm, 128), tab(2)),
        ],
        out_specs=[pl.BlockSpec((tm, w), row) for w in widths],
        out_shape=[jax.ShapeDtypeStruct((rows, w), d) for w, d in zip(widths, dtypes)],
        compiler_params=pltpu.CompilerParams(
            dimension_semantics=("parallel",), vmem_limit_bytes=VMEM_LIMIT),
        name="inproj",
    )(h, gain, w_in, tables, tables, tables)


def _rope_tables(pos):
    half = ROT_DIM // 2
    inv_freq = 1.0 / (ROPE_THETA ** (jnp.arange(half, dtype=F32) * (2.0 / ROT_DIM)))
    ang = pos[:, None] * inv_freq[None, :]
    trig = jnp.concatenate([jnp.cos(ang), jnp.sin(ang)], axis=1)
    lane = np.arange(128) % HEAD_DIM
    sel = np.zeros((2 * half, 3 * 128), np.float32)
    bias = np.zeros((1, 3 * 128), np.float32)
    for l in range(128):
        j = lane[l]
        if j < ROT_DIM:
            sel[j % half, l] = 1.0
        else:
            bias[0, l] = 1.0
        if j < half:
            sel[half + j, 128 + l] = -1.0
        elif j < ROT_DIM:
            sel[half + j - half, 256 + l] = 1.0
    return jnp.dot(trig, jnp.asarray(sel), precision=lax.Precision.HIGHEST) + jnp.asarray(bias)


ATT_TILE_BLOCKS = 4


def _attn_kernel(sink_ref, q_ref, k_ref, v_ref, kp_ref, vp_ref, km_ref, vm_ref, o_ref):
    first = pl.program_id(1) == 0
    k_prev = jnp.where(first, km_ref[...], kp_ref[...])
    v_prev = jnp.where(first, vm_ref[...], vp_ref[...])
    k_all = jnp.concatenate([k_prev, k_ref[...]], axis=0)
    v_all = jnp.concatenate([v_prev, v_ref[...]], axis=0)
    row = lax.broadcasted_iota(jnp.int32, (BLOCK, 2 * BLOCK), 0)
    col = lax.broadcasted_iota(jnp.int32, (BLOCK, 2 * BLOCK), 1)
    mask = (col > row) & (col <= row + WINDOW)
    mask_first = (col > jnp.where(first, jnp.maximum(row, META_PAD - 1), row)) & (col <= row + WINDOW)
    sink_col = col == 0
    sink_row = lax.broadcasted_iota(jnp.int32, (2 * BLOCK, ATT_KV), 0) == 0

    pairs = [(j, g) for j in range(ATT_TILE_BLOCKS) for g in range(ATT_KV_HEADS)]
    scores = {}
    for j, g in pairs:
        qj = q_ref[j * BLOCK:(j + 1) * BLOCK, :]
        kg = k_all[j * BLOCK:(j + 2) * BLOCK, g * HEAD_DIM:(g + 1) * HEAD_DIM]
        q4 = jnp.concatenate([qj[:, hd * HEAD_DIM:(hd + 1) * HEAD_DIM]
                              for hd in range(g * ATT_GROUP, (g + 1) * ATT_GROUP)], axis=0)
        scores[j, g] = lax.dot_general(q4, kg, (((1,), (1,)), ((), ())),
                                       preferred_element_type=F32)
    logits, maxima = {}, {}
    for j, g in pairs:
        mj = mask_first if j == 0 else mask
        for i in range(ATT_GROUP):
            sh = jnp.where(mj, scores[j, g][i * BLOCK:(i + 1) * BLOCK], MASK_VALUE)
            sh = jnp.where(sink_col, sink_ref[g * ATT_GROUP + i] * LOG2_E, sh)
            logits[j, g, i] = sh
            maxima[j, g, i] = jnp.max(sh, axis=-1, keepdims=True)
    probs, denoms = {}, {}
    for key, sh in logits.items():
        p = jnp.exp2(sh - maxima[key])
        denoms[key] = jnp.sum(p, axis=-1, keepdims=True)
        probs[key] = p.astype(BF16)
    for j in range(ATT_TILE_BLOCKS):
        vj = jnp.where(sink_row, 0.0, v_all[j * BLOCK:(j + 2) * BLOCK].astype(F32)).astype(BF16)
        outs = []
        for g in range(ATT_KV_HEADS):
            p4 = jnp.concatenate([probs[j, g, i] for i in range(ATT_GROUP)], axis=0)
            o4 = _dot(p4, vj[:, g * HEAD_DIM:(g + 1) * HEAD_DIM])
            outs += [o4[i * BLOCK:(i + 1) * BLOCK] / denoms[j, g, i] for i in range(ATT_GROUP)]
        o_ref[j * BLOCK:(j + 1) * BLOCK, :] = jnp.concatenate(outs, axis=1).astype(BF16)


def _attention(sinks, q, k, v, k_meta, v_meta, batch, nblk):
    tile = ATT_TILE_BLOCKS * BLOCK
    ntile = nblk // ATT_TILE_BLOCKS

    def own(b, n):
        return (b * ntile + n, 0)

    def prev(b, n):
        return (b * nblk + jnp.maximum(n * ATT_TILE_BLOCKS - 1, 0), 0)

    return pl.pallas_call(
        _attn_kernel,
        grid=(batch, ntile),
        in_specs=[
            pl.BlockSpec(memory_space=pltpu.SMEM),
            pl.BlockSpec((tile, ATT_Q), own),
            pl.BlockSpec((tile, ATT_KV), own),
            pl.BlockSpec((tile, ATT_KV), own),
            pl.BlockSpec((BLOCK, ATT_KV), prev),
            pl.BlockSpec((BLOCK, ATT_KV), prev),
            _const_spec((BLOCK, ATT_KV)),
            _const_spec((BLOCK, ATT_KV)),
        ],
        out_specs=pl.BlockSpec((tile, ATT_Q), own),
        out_shape=jax.ShapeDtypeStruct((batch * nblk * BLOCK, ATT_Q), BF16),
        compiler_params=pltpu.CompilerParams(
            dimension_semantics=("parallel", "parallel"), vmem_limit_bytes=VMEM_LIMIT),
        name="attention",
    )(sinks, q, k, v, k, v, k_meta, v_meta)


O_K = RWKV_WIDTH
O_V = 2 * RWKV_WIDTH
O_W = 3 * RWKV_WIDTH
O_A = O_W + DECAY_LORA
O_G = O_A + AAA_LORA

GROUP_HEADS = 4
GROUP_W = GROUP_HEADS * RWKV_HEAD
N_GROUPS = RWKV_HEADS // GROUP_HEADS
CUMSUM_PARTS = 2
INV_DOUBLINGS = 4
STAGE_SKEW = 1


def _mm(a, b):
    return jnp.dot(a.astype(BF16), b.astype(BF16), preferred_element_type=F32)


def _mm_nt(a, b):
    return lax.dot_general(a.astype(BF16), b.astype(BF16), (((1,), (1,)), ((), ())),
                           preferred_element_type=F32)


def _mm_tn(a, b):
    return lax.dot_general(a, b, (((0,), (0,)), ((), ())), preferred_element_type=F32)


def _sigmoid(x):
    return 0.5 * jnp.tanh(0.5 * x) + 0.5


DIAG_TILES = [(slice(h * RWKV_HEAD, (h + 1) * RWKV_HEAD), slice((h // 2) * 128, (h // 2 + 1) * 128))
              for h in range(GROUP_HEADS)]


def _diag_tiles(x, bd_mask):
    return [x[:, lanes] * bd_mask[rows, lanes] for rows, lanes in DIAG_TILES]


def _from_diag_tiles(tiles):
    zero = jnp.zeros_like(tiles[0])
    return jnp.concatenate(
        [jnp.concatenate([t, zero] if h // 2 == 0 else [zero, t], axis=1)
         for h, t in enumerate(tiles)], axis=0)


def _block_diag(x, bd_mask):
    return _from_diag_tiles(_diag_tiles(x.astype(BF16), bd_mask))


def _head_sum(x, bd_mask):
    return jnp.concatenate(
        [_mm(x[:, i * GROUP_W:(i + 1) * GROUP_W], bd_mask) for i in range(N_GROUPS)], axis=1)


def _chunk_cumsum(tri, x):
    pieces, rem = [], x
    for i in range(CUMSUM_PARTS):
        hi = rem.astype(BF16)
        pieces.append(hi)
        if i + 1 < CUMSUM_PARTS:
            rem = rem - hi.astype(F32)
    out = jnp.dot(tri, jnp.concatenate(pieces, axis=1), preferred_element_type=F32)
    w = x.shape[1]
    return sum(out[:, i * w:(i + 1) * w] for i in range(CUMSUM_PARTS))


def _rwkv_kernel(zr_ref, zprev0_ref, s0_ref, tri_ref, bd_ref, mu_ref, w0_ref, w2_ref,
                 a0_ref, a2_ref, g2_ref, kk_ref, ka_ref, rk_ref, lnw_ref, lnb_ref,
                 y_ref, sout_ref, state_scr, prev_scr, ybuf):
    nb = zr_ref.shape[0]
    n = pl.program_id(0)

    @pl.when(n == 0)
    def _():
        for b in range(nb):
            state_scr[b] = s0_ref[...]
            prev_scr[b] = zprev0_ref[...]

    bd_mask = bd_ref[...]
    bd_f32 = bd_mask.astype(F32)
    tri = tri_ref[...]
    row_id = lax.broadcasted_iota(jnp.int32, (BLOCK, RWKV_IN), 0)
    ti = lax.broadcasted_iota(jnp.int32, (CHUNK, GROUP_W), 0)
    si = lax.broadcasted_iota(jnp.int32, (CHUNK, GROUP_W), 1) % CHUNK
    strict = ti > si
    incl = ti >= si
    eye = (ti == si).astype(F32)
    n_chunks = BLOCK // CHUNK
    groups = range(N_GROUPS)
    chains = [(c, gi) for c in range(n_chunks) for gi in groups]

    def batch_stages(b):
        z = zr_ref[b]
        z_prev = jnp.where(row_id == 0, prev_scr[b], pltpu.roll(z, 1, 0))
        prev_scr[b] = z[BLOCK - 1:BLOCK, :]
        zs = z + (z_prev - z) * mu_ref[...]
        r = zs[:, :O_K]
        k = zs[:, O_K:O_V]
        v = zs[:, O_V:O_W]
        ld = -DECAY_SCALE * _sigmoid(w0_ref[...] + _mm(jnp.tanh(zs[:, O_W:O_A]), w2_ref[...]))
        a = _sigmoid(a0_ref[...] + _mm(zs[:, O_A:O_G], a2_ref[...]))
        g = _mm(_sigmoid(zs[:, O_G:]), g2_ref[...])
        kk = k * kk_ref[...]
        kk_sq = _head_sum(kk * kk, bd_mask)
        yield
        kk = kk * lax.rsqrt(jnp.maximum(kk_sq, 1e-24))
        k = k * (1.0 + (a - 1.0) * ka_ref[...])
        cum = _chunk_cumsum(tri, ld)
        yield
        e_pos = jnp.exp(cum)
        e_neg = jnp.exp(-cum)
        r_t = r * e_pos
        k_t = k * e_neg
        a_t = -kk * jnp.exp(cum - ld)
        b_t = kk * a * e_neg

        def part(x, ch):
            c, gi = ch
            return x[c * CHUNK:(c + 1) * CHUNK, gi * GROUP_W:(gi + 1) * GROUP_W]

        ar, m_ab, m_ak, m_rb, m_rk = {}, {}, {}, {}, {}
        for ch in chains:
            ar[ch] = jnp.concatenate([part(a_t, ch), part(r_t, ch)], axis=0).astype(BF16)
            bd_bk = jnp.concatenate([_block_diag(part(b_t, ch), bd_mask),
                                     _block_diag(part(k_t, ch), bd_mask)], axis=0)
            sc = _mm_nt(ar[ch], bd_bk)
            m_ab[ch] = jnp.where(strict, sc[:CHUNK, :GROUP_W], 0.0)
            m_ak[ch] = jnp.where(strict, sc[:CHUNK, GROUP_W:], 0.0)
            m_rb[ch] = jnp.where(incl, sc[CHUNK:, :GROUP_W], 0.0)
            m_rk[ch] = jnp.where(incl, sc[CHUNK:, GROUP_W:], 0.0)
        yield
        inv = {ch: eye + m_ab[ch] for ch in chains}
        pw = {ch: _mm(m_ab[ch], _block_diag(m_ab[ch], bd_mask)) for ch in chains}
        yield
        for _ in range(INV_DOUBLINGS):
            for ch in chains:
                both = _mm(jnp.concatenate([pw[ch], inv[ch]], axis=0), _block_diag(pw[ch], bd_mask))
                pw[ch] = both[:CHUNK]
                inv[ch] = inv[ch] + both[CHUNK:]
            yield
        for ch in chains:
            inv[ch] = inv[ch] + _mm(inv[ch], _block_diag(pw[ch], bd_mask))
        mv = {ch: _mm(jnp.concatenate([m_ak[ch], m_rk[ch]], axis=0),
                      _block_diag(part(v, ch), bd_mask)) for ch in chains}
        yield
        state = {gi: [state_scr[(b, gi) + tile] for tile in DIAG_TILES] for gi in groups}
        for c in range(n_chunks):
            cur = [(c, gi) for gi in groups]
            xr0 = {ch: _mm_nt(ar[ch], _from_diag_tiles([t.astype(BF16) for t in state[ch[1]]]))
                   for ch in cur}
            yield
            u = {ch: _mm(inv[ch], _block_diag(xr0[ch][:CHUNK] + mv[ch][:CHUNK], bd_mask))
                 for ch in cur}
            yield
            for ch in cur:
                gi = ch[1]
                y = xr0[ch][CHUNK:] + mv[ch][CHUNK:] + _mm(m_rb[ch], _block_diag(u[ch], bd_mask))
                upd = _mm_tn(jnp.concatenate([u[ch], part(v, ch)], axis=0),
                             jnp.concatenate([part(b_t, ch), part(k_t, ch)], axis=0))
                last = (c + 1) * CHUNK - 1
                w_end = e_pos[last:last + 1, gi * GROUP_W:(gi + 1) * GROUP_W]
                state[gi] = [(s + upd[tile] * bd_f32[tile]) * w_end[:, tile[1]]
                             for s, tile in zip(state[gi], DIAG_TILES)]
                ybuf[b, c * CHUNK:(c + 1) * CHUNK, gi * GROUP_W:(gi + 1) * GROUP_W] = y
            yield
        for gi in groups:
            for s, tile in zip(state[gi], DIAG_TILES):
                state_scr[(b, gi) + tile] = s
            if b == 0:
                sout_ref[gi] = _from_diag_tiles(state[gi])
        y = ybuf[b]
        mean = _head_sum(y, bd_mask) * (1.0 / RWKV_HEAD)
        bonus = _head_sum(r * k * rk_ref[...], bd_mask) * v
        yield
        yc = y - mean
        var = _head_sum(yc * yc, bd_mask) * (1.0 / RWKV_HEAD)
        yield
        yn = yc * lax.rsqrt(var + GN_EPS) * lnw_ref[...] + lnb_ref[...]
        y_ref[b] = ((yn + bonus) * g).astype(BF16)

    stages = [batch_stages(b) for b in range(nb)]
    live = set(range(nb))
    tick = 0
    while live:
        for b in sorted(live):
            if tick >= b * STAGE_SKEW and next(stages[b], "done") == "done":
                live.discard(b)
        tick += 1


def _rwkv(zr, zprev0, s0, tri, bd_mask, params):
    batch, rows, _ = zr.shape
    state_shape = (N_GROUPS, GROUP_W, GROUP_W)
    return pl.pallas_call(
        _rwkv_kernel,
        grid=(rows // BLOCK,),
        in_specs=[
            pl.BlockSpec((batch, BLOCK, RWKV_IN), lambda n: (0, n, 0)),
            _const_spec((1, RWKV_IN)),
            _const_spec(state_shape),
            _const_spec((BLOCK, BLOCK)),
            _const_spec((GROUP_W, GROUP_W)),
        ] + [_const_spec(p.shape) for p in params],
        out_specs=[
            pl.BlockSpec((batch, BLOCK, RWKV_WIDTH), lambda n: (0, n, 0)),
            pl.BlockSpec(state_shape, lambda n: (0, 0, 0)),
        ],
        out_shape=[
            jax.ShapeDtypeStruct((batch, rows, RWKV_WIDTH), BF16),
            jax.ShapeDtypeStruct(state_shape, F32),
        ],
        scratch_shapes=[
            pltpu.VMEM((batch,) + state_shape, F32),
            pltpu.VMEM((batch, 1, RWKV_IN), F32),
            pltpu.VMEM((batch, BLOCK, RWKV_WIDTH), F32),
        ],
        compiler_params=pltpu.CompilerParams(
            dimension_semantics=("arbitrary",), vmem_limit_bytes=VMEM_LIMIT),
        name="rwkv",
    )(zr, zprev0, s0, tri, bd_mask, *params)


def kernel(x, meta_tokens, ffn1_norm_pre, ffn1_w_gate_up, ffn1_w_down, ffn1_norm_post, mix_norm_pre, w_in, att_sinks, rwkv_mu, rwkv_w0, rwkv_w2, rwkv_a0, rwkv_a2, rwkv_g2, rwkv_k_k, rwkv_k_a, rwkv_r_k, rwkv_ln_w, rwkv_ln_b, w_att_branch, w_rwkv_branch, w_mix_out, mix_norm_post, ffn2_norm_pre, ffn2_w_gate_up, ffn2_w_down, ffn2_norm_post):
    batch, seq, _ = x.shape
    nblk = seq // BLOCK
    depth = w_in.shape[0]
    assert depth == 1 and seq % BLOCK == 0

    tab_x = _rope_tables(jnp.arange(N_META, N_META + seq, dtype=jnp.int32).astype(F32))
    pos_m = jnp.maximum(jnp.arange(BLOCK, dtype=jnp.int32) - META_PAD, 0).astype(F32)
    tab_m = _rope_tables(pos_m)
    ri = jnp.arange(BLOCK)
    tri = ((ri[:, None] >= ri[None, :]) &
           (ri[:, None] // CHUNK == ri[None, :] // CHUNK)).astype(BF16)
    li = jnp.arange(GROUP_W) // RWKV_HEAD
    bd_mask = (li[:, None] == li[None, :]).astype(BF16)

    hx = x.reshape(batch * seq, D_MODEL)
    hm = jnp.concatenate([jnp.zeros((META_PAD, D_MODEL), x.dtype),
                          meta_tokens.astype(x.dtype)], axis=0)

    def row(p):
        return p.reshape(1, -1)

    l = 0
    wgu1, wd1 = ffn1_w_gate_up[l].astype(BF16), ffn1_w_down[l].astype(BF16)
    wgu2, wd2 = ffn2_w_gate_up[l].astype(BF16), ffn2_w_down[l].astype(BF16)
    w_in_l = w_in[l].astype(BF16)
    wa, wr, wo = (w_att_branch[l].astype(BF16), w_rwkv_branch[l].astype(BF16),
                  w_mix_out[l].astype(BF16))
    rw_params = (row(rwkv_mu[l]), row(rwkv_w0[l]), rwkv_w2[l], row(rwkv_a0[l]), rwkv_a2[l],
                 rwkv_g2[l], row(rwkv_k_k[l]), row(rwkv_k_a[l]), row(rwkv_r_k[l]),
                 row(rwkv_ln_w[l]), row(rwkv_ln_b[l]))

    outs = []
    for h, tabs in ((hm, tab_m), (hx, tab_x)):
        h1 = _ffn(h, row(ffn1_norm_pre[l]), wgu1, wd1, row(ffn1_norm_post[l]))
        outs.append((h1,) + tuple(_inproj(h1, row(mix_norm_pre[l]), w_in_l, tabs)))
    (_, _, km, vm, zrm, _, _), (hx1, q, k, v, zr, ga, gr) = outs

    o_att = _attention(att_sinks[l], q, k, v, km, vm, batch, nblk)

    zero_state = jnp.zeros((N_GROUPS, GROUP_W, GROUP_W), F32)
    _, s_meta = _rwkv(zrm[None], jnp.zeros((1, RWKV_IN), F32), zero_state, tri, bd_mask,
                      rw_params)
    y_rwkv, _ = _rwkv(zr.reshape(batch, seq, RWKV_IN), zrm[BLOCK - 1:], s_meta, tri, bd_mask,
                      rw_params)
    y_rwkv = y_rwkv.reshape(batch * seq, RWKV_WIDTH)

    hx = _ffn(hx1, row(ffn2_norm_pre[l]), wgu2, wd2, row(ffn2_norm_post[l]),
              merge=(o_att, y_rwkv, ga, gr, wa, wr, wo, row(mix_norm_post[l])))
    return hx.reshape(batch, seq, D_MODEL)
```

```python
import functools

import jax
import jax.numpy as jnp
import numpy as np
from jax import lax
from jax.experimental import pallas as pl
from jax.experimental.pallas import tpu as pltpu

D_MODEL = 1024
N_META = 16
ATT_HEADS = 8
ATT_KV_HEADS = 2
ATT_GROUP = ATT_HEADS // ATT_KV_HEADS
HEAD_DIM = 64
WINDOW = 128
BLOCK = 128
ROPE_THETA = 500000.0
ROT_DIM = HEAD_DIM // 4
ATT_Q = ATT_HEADS * HEAD_DIM
ATT_KV = ATT_KV_HEADS * HEAD_DIM
MASK_VALUE = -1e30
RWKV_HEADS = 8
RWKV_HEAD = 64
RWKV_WIDTH = RWKV_HEADS * RWKV_HEAD
DECAY_LORA = 64
AAA_LORA = 64
GATE_LORA = 128
GN_EPS = 64e-5
RWKV_IN = 3 * RWKV_WIDTH + DECAY_LORA + AAA_LORA + GATE_LORA
D_FF = 2816
NORM_EPS = 1e-6

META_PAD = BLOCK - N_META
CHUNK = 64
DECAY_SCALE = 0.6065306597126334
LOG2_E = 1.4426950408889634

VMEM_LIMIT = 56 * 1024 * 1024
ROW_TILE = 512
SUB_TILES = 2

F32 = jnp.float32
BF16 = jnp.bfloat16


def _rms(x, gain):
    ms = jnp.mean(x * x, axis=-1, keepdims=True)
    return x * lax.rsqrt(ms + NORM_EPS) * gain


def _dot(a, b):
    return jnp.dot(a, b, preferred_element_type=F32)


def _const_spec(shape):
    nd = len(shape)
    return pl.BlockSpec(shape, lambda *_: (0,) * nd, pipeline_mode=pl.Buffered(1))


def _sub_tiles(rows):
    sub = rows // SUB_TILES
    return [slice(i * sub, (i + 1) * sub) for i in range(SUB_TILES)]


def _ffn_kernel(*refs, with_merge):
    if with_merge:
        (h_ref, oa_ref, yr_ref, ga_ref, gr_ref, wa_ref, wr_ref, wo_ref, gmix_ref,
         gpre_ref, wgu_ref, wd_ref, gpost_ref, o_ref) = refs
    else:
        h_ref, gpre_ref, wgu_ref, wd_ref, gpost_ref, o_ref = refs
    parts = _sub_tiles(h_ref.shape[0])
    if with_merge:
        merged = []
        for p in parts:
            y_att = _dot(oa_ref[p, :], wa_ref[...])
            y_rwkv = _dot(yr_ref[p, :], wr_ref[...])
            merged.append((ga_ref[p, :].astype(F32) * y_att
                           + gr_ref[p, :].astype(F32) * y_rwkv).astype(BF16))
        hs = [h_ref[p, :] + _rms(_dot(m, wo_ref[...]), gmix_ref[...])
              for p, m in zip(parts, merged)]
    else:
        hs = [h_ref[p, :] for p in parts]
    acts = []
    for h in hs:
        hn = _rms(h, gpre_ref[...]).astype(BF16)
        gate = _dot(hn, wgu_ref[:, :D_FF])
        up = _dot(hn, wgu_ref[:, D_FF:])
        acts.append((gate * jax.nn.sigmoid(gate) * up).astype(BF16))
    for p, h, act in zip(parts, hs, acts):
        f = _dot(act, wd_ref[...])
        o_ref[p, :] = h + 0.5 * _rms(f, gpost_ref[...])


def _ffn(h, gpre, wgu, wd, gpost, merge=None):
    rows = h.shape[0]
    tm = min(ROW_TILE, rows)

    def row(i):
        return (i, 0)

    operands = [h]
    in_specs = [pl.BlockSpec((tm, D_MODEL), row)]
    if merge is not None:
        o_att, y_rwkv, ga, gr, wa, wr, wo, gmix = merge
        operands += [o_att, y_rwkv, ga, gr, wa, wr, wo, gmix]
        in_specs += [
            pl.BlockSpec((tm, ATT_Q), row),
            pl.BlockSpec((tm, RWKV_WIDTH), row),
            pl.BlockSpec((tm, D_MODEL), row),
            pl.BlockSpec((tm, D_MODEL), row),
            _const_spec((ATT_Q, D_MODEL)),
            _const_spec((RWKV_WIDTH, D_MODEL)),
            _const_spec((D_MODEL, D_MODEL)),
            _const_spec((1, D_MODEL)),
        ]
    operands += [gpre, wgu, wd, gpost]
    in_specs += [
        _const_spec((1, D_MODEL)),
        _const_spec((D_MODEL, 2 * D_FF)),
        _const_spec((D_FF, D_MODEL)),
        _const_spec((1, D_MODEL)),
    ]
    return pl.pallas_call(
        functools.partial(_ffn_kernel, with_merge=merge is not None),
        grid=(rows // tm,),
        in_specs=in_specs,
        out_specs=pl.BlockSpec((tm, D_MODEL), row),
        out_shape=jax.ShapeDtypeStruct((rows, D_MODEL), F32),
        compiler_params=pltpu.CompilerParams(
            dimension_semantics=("parallel",), vmem_limit_bytes=VMEM_LIMIT),
        name="merge_ffn" if merge is not None else "ffn",
    )(*operands)


C_Q = ATT_Q
C_K = C_Q + ATT_KV
C_V = C_K + ATT_KV
C_R = C_V + RWKV_IN
C_GA = C_R + D_MODEL
IN_COLS = C_GA + D_MODEL


def _rope(x, cos_t, sin_lo, sin_hi):
    n = x.shape[1] // 128
    if n > 1:
        cos_t = jnp.concatenate([cos_t] * n, axis=1)
        sin_lo = jnp.concatenate([sin_lo] * n, axis=1)
        sin_hi = jnp.concatenate([sin_hi] * n, axis=1)
    width = x.shape[1]
    half = ROT_DIM // 2
    from_hi = pltpu.roll(x, width - half, 1)
    from_lo = pltpu.roll(x, half, 1)
    return x * cos_t + from_hi * sin_lo + from_lo * sin_hi


def _inproj_kernel(h_ref, g_ref, w_ref, rope_ref, cosr_ref, sinr_ref, mu_ref,
                   q_ref, k_ref, v_ref, zs_ref, zfirst_ref, zlast_ref, ga_ref, gr_ref,
                   *, pos_base, tiles_per_seq):
    tm = h_ref.shape[0]
    tile_pos = pos_base + (pl.program_id(0) % tiles_per_seq) * tm
    freq, sign_lo, sign_hi = rope_ref[0:1, :], rope_ref[1:2, :], rope_ref[2:3, :]
    carry = None
    for p in _sub_tiles(tm):
        rows = p.stop - p.start
        u = _rms(h_ref[p, :], g_ref[...]).astype(BF16)
        z = _dot(u, w_ref[:, C_V:C_R])
        z_prev = pltpu.roll(z, 1, 0)
        if carry is None:
            zfirst_ref[...] = z[0:1, :]
        else:
            row_id = lax.broadcasted_iota(jnp.int32, z.shape, 0)
            z_prev = jnp.where(row_id == 0, carry, z_prev)
        zs_ref[p, :] = z + (z_prev - z) * mu_ref[...]
        carry = z[rows - 1:rows, :]
        ga_ref[p, :] = jax.nn.sigmoid(_dot(u, w_ref[:, C_R:C_GA])).astype(BF16)
        gr_ref[p, :] = jax.nn.sigmoid(_dot(u, w_ref[:, C_GA:])).astype(BF16)
        v_ref[p, :] = _dot(u, w_ref[:, C_K:C_V]).astype(BF16)
        base = jnp.full((1, 128), tile_pos + p.start, jnp.int32).astype(F32) * freq
        cos_b, sin_b = jnp.cos(base), jnp.sin(base)
        cos_r, sin_r = cosr_ref[...], sinr_ref[...]
        cos_t = cos_b * cos_r - sin_b * sin_r
        sin = sin_b * cos_r + cos_b * sin_r
        sin_lo, sin_hi = sin * sign_lo, sin * sign_hi
        q = _dot(u, w_ref[:, :C_Q])
        q_ref[p, :] = (_rope(q, cos_t, sin_lo, sin_hi) * (LOG2_E * HEAD_DIM ** -0.5)).astype(BF16)
        k = _dot(u, w_ref[:, C_Q:C_K])
        k_ref[p, :] = _rope(k, cos_t, sin_lo, sin_hi).astype(BF16)
    zlast_ref[...] = carry


def _inproj(h, gain, w_in, rope_consts, mu, pos_base, seq_rows):
    rope, cos_r, sin_r = rope_consts
    rows = h.shape[0]
    tm = min(ROW_TILE, rows)
    ntiles = rows // tm
    sub_rows = tm // SUB_TILES

    def row(i):
        return (i, 0)

    widths = (ATT_Q, ATT_KV, ATT_KV, RWKV_IN, D_MODEL, D_MODEL)
    dtypes = (BF16, BF16, BF16, F32, BF16, BF16)
    edge_spec = pl.BlockSpec((None, 1, RWKV_IN), lambda i: (i, 0, 0))
    edge_shape = jax.ShapeDtypeStruct((ntiles, 1, RWKV_IN), F32)
    out_specs = [pl.BlockSpec((tm, w), row) for w in widths]
    out_shape = [jax.ShapeDtypeStruct((rows, w), d) for w, d in zip(widths, dtypes)]
    q, k, v, zs, zfirst, zlast, ga, gr = pl.pallas_call(
        functools.partial(_inproj_kernel, pos_base=pos_base, tiles_per_seq=seq_rows // tm),
        grid=(ntiles,),
        in_specs=[
            pl.BlockSpec((tm, D_MODEL), row),
            _const_spec((1, D_MODEL)),
            _const_spec((D_MODEL, IN_COLS)),
            _const_spec((8, 128)),
            _const_spec((sub_rows, 128)),
            _const_spec((sub_rows, 128)),
            _const_spec((1, RWKV_IN)),
        ],
        out_specs=out_specs[:4] + [edge_spec, edge_spec] + out_specs[4:],
        out_shape=out_shape[:4] + [edge_shape, edge_shape] + out_shape[4:],
        compiler_params=pltpu.CompilerParams(
            dimension_semantics=("parallel",), vmem_limit_bytes=VMEM_LIMIT),
        name="inproj",
    )(h, gain, w_in, rope, cos_r[:sub_rows], sin_r[:sub_rows], mu)
    return q, k, v, (zs, zfirst, zlast, tm // BLOCK), ga, gr


def _rope_consts():
    half = ROT_DIM // 2
    inv_freq = 1.0 / (ROPE_THETA ** (jnp.arange(half, dtype=F32) * (2.0 / ROT_DIM)))
    lane = np.arange(128) % HEAD_DIM
    freq = jnp.where(lane < ROT_DIM, inv_freq[lane % half], 0.0)
    sign_lo = jnp.asarray(np.where(lane < half, -1.0, 0.0), F32)
    sign_hi = jnp.asarray(np.where((lane >= half) & (lane < ROT_DIM), 1.0, 0.0), F32)
    rope = jnp.concatenate([jnp.stack([freq, sign_lo, sign_hi]), jnp.zeros((5, 128), F32)], axis=0)
    offs = jnp.arange(ROW_TILE // SUB_TILES, dtype=F32)[:, None] * freq[None, :]
    return rope, jnp.cos(offs), jnp.sin(offs)


ATT_TILE_BLOCKS = 4


def _attn_kernel(sink_ref, q_ref, k_ref, v_ref, kp_ref, vp_ref, km_ref, vm_ref, o_ref):
    first = pl.program_id(1) == 0
    k_prev = jnp.where(first, km_ref[...], kp_ref[...])
    v_prev = jnp.where(first, vm_ref[...], vp_ref[...])
    k_all = jnp.concatenate([k_prev, k_ref[...]], axis=0)
    v_all = jnp.concatenate([v_prev, v_ref[...]], axis=0)
    row = lax.broadcasted_iota(jnp.int32, (BLOCK, 2 * BLOCK), 0)
    col = lax.broadcasted_iota(jnp.int32, (BLOCK, 2 * BLOCK), 1)
    mask = (col > row) & (col <= row + WINDOW)
    mask_first = (col > jnp.where(first, jnp.maximum(row, META_PAD - 1), row)) & (col <= row + WINDOW)
    sink_col = col == 0
    sink_row = lax.broadcasted_iota(jnp.int32, (2 * BLOCK, ATT_KV), 0) == 0

    pairs = [(j, g) for j in range(ATT_TILE_BLOCKS) for g in range(ATT_KV_HEADS)]
    scores = {}
    for j, g in pairs:
        qj = q_ref[j * BLOCK:(j + 1) * BLOCK, :]
        kg = k_all[j * BLOCK:(j + 2) * BLOCK, g * HEAD_DIM:(g + 1) * HEAD_DIM]
        q4 = jnp.concatenate([qj[:, hd * HEAD_DIM:(hd + 1) * HEAD_DIM]
                              for hd in range(g * ATT_GROUP, (g + 1) * ATT_GROUP)], axis=0)
        scores[j, g] = lax.dot_general(q4, kg, (((1,), (1,)), ((), ())),
                                       preferred_element_type=F32)
    logits, maxima = {}, {}
    for j, g in pairs:
        mj = mask_first if j == 0 else mask
        for i in range(ATT_GROUP):
            sh = jnp.where(mj, scores[j, g][i * BLOCK:(i + 1) * BLOCK], MASK_VALUE)
            sh = jnp.where(sink_col, sink_ref[g * ATT_GROUP + i] * LOG2_E, sh)
            logits[j, g, i] = sh
            maxima[j, g, i] = jnp.max(sh, axis=-1, keepdims=True)
    probs, denoms = {}, {}
    for key, sh in logits.items():
        p = jnp.exp2(sh - maxima[key])
        denoms[key] = jnp.sum(p, axis=-1, keepdims=True)
        probs[key] = p.astype(BF16)
    for j in range(ATT_TILE_BLOCKS):
        vj = jnp.where(sink_row, 0.0, v_all[j * BLOCK:(j + 2) * BLOCK].astype(F32)).astype(BF16)
        outs = []
        for g in range(ATT_KV_HEADS):
            p4 = jnp.concatenate([probs[j, g, i] for i in range(ATT_GROUP)], axis=0)
            o4 = _dot(p4, vj[:, g * HEAD_DIM:(g + 1) * HEAD_DIM])
            outs += [o4[i * BLOCK:(i + 1) * BLOCK] / denoms[j, g, i] for i in range(ATT_GROUP)]
        o_ref[j * BLOCK:(j + 1) * BLOCK, :] = jnp.concatenate(outs, axis=1).astype(BF16)


def _attention(sinks, q, k, v, k_meta, v_meta, batch, nblk):
    tile = ATT_TILE_BLOCKS * BLOCK
    ntile = nblk // ATT_TILE_BLOCKS

    def own(b, n):
        return (b * ntile + n, 0)

    def prev(b, n):
        return (b * nblk + jnp.maximum(n * ATT_TILE_BLOCKS - 1, 0), 0)

    return pl.pallas_call(
        _attn_kernel,
        grid=(batch, ntile),
        in_specs=[
            pl.BlockSpec(memory_space=pltpu.SMEM),
            pl.BlockSpec((tile, ATT_Q), own),
            pl.BlockSpec((tile, ATT_KV), own),
            pl.BlockSpec((tile, ATT_KV), own),
            pl.BlockSpec((BLOCK, ATT_KV), prev),
            pl.BlockSpec((BLOCK, ATT_KV), prev),
            _const_spec((BLOCK, ATT_KV)),
            _const_spec((BLOCK, ATT_KV)),
        ],
        out_specs=pl.BlockSpec((tile, ATT_Q), own),
        out_shape=jax.ShapeDtypeStruct((batch * nblk * BLOCK, ATT_Q), BF16),
        compiler_params=pltpu.CompilerParams(
            dimension_semantics=("parallel", "parallel"), vmem_limit_bytes=VMEM_LIMIT),
        name="attention",
    )(sinks, q, k, v, k, v, k_meta, v_meta)


O_K = RWKV_WIDTH
O_V = 2 * RWKV_WIDTH
O_W = 3 * RWKV_WIDTH
O_A = O_W + DECAY_LORA
O_G = O_A + AAA_LORA

GROUP_HEADS = 4
GROUP_W = GROUP_HEADS * RWKV_HEAD
N_GROUPS = RWKV_HEADS // GROUP_HEADS
CUMSUM_PARTS = 2
INV_DOUBLINGS = 4
STAGE_SKEW = 1


def _mm(a, b):
    return jnp.dot(a.astype(BF16), b.astype(BF16), preferred_element_type=F32)


def _mm_nt(a, b):
    return lax.dot_general(a.astype(BF16), b.astype(BF16), (((1,), (1,)), ((), ())),
                           preferred_element_type=F32)


def _mm_tn(a, b):
    return lax.dot_general(a, b, (((0,), (0,)), ((), ())), preferred_element_type=F32)


def _sigmoid(x):
    return 0.5 * jnp.tanh(0.5 * x) + 0.5


DIAG_TILES = [(slice(h * RWKV_HEAD, (h + 1) * RWKV_HEAD), slice((h // 2) * 128, (h // 2 + 1) * 128))
              for h in range(GROUP_HEADS)]


def _diag_tiles(x, bd_mask):
    return [x[:, lanes] * bd_mask[rows, lanes] for rows, lanes in DIAG_TILES]


def _from_diag_tiles(tiles):
    zero = jnp.zeros_like(tiles[0])
    return jnp.concatenate(
        [jnp.concatenate([t, zero] if h // 2 == 0 else [zero, t], axis=1)
         for h, t in enumerate(tiles)], axis=0)


def _block_diag(x, bd_mask):
    return _from_diag_tiles(_diag_tiles(x.astype(BF16), bd_mask))


def _head_sum(x, bd_mask):
    return jnp.concatenate(
        [_mm(x[:, i * GROUP_W:(i + 1) * GROUP_W], bd_mask) for i in range(N_GROUPS)], axis=1)


def _chunk_cumsum(tri, x):
    pieces, rem = [], x
    for i in range(CUMSUM_PARTS):
        hi = rem.astype(BF16)
        pieces.append(hi)
        if i + 1 < CUMSUM_PARTS:
            rem = rem - hi.astype(F32)
    out = jnp.dot(tri, jnp.concatenate(pieces, axis=1), preferred_element_type=F32)
    w = x.shape[1]
    return sum(out[:, i * w:(i + 1) * w] for i in range(CUMSUM_PARTS))


def _rwkv_kernel(zs_ref, zfirst_ref, zlast_ref, zprev0_ref, s0_ref, tri_ref, bd_ref, mu_ref, w0_ref,
                 w2_ref, a0_ref, a2_ref, g2_ref, kk_ref, ka_ref, rk_ref, lnw_ref, lnb_ref,
                 y_ref, sout_ref, state_scr, ybuf, *, blocks_per_tile):
    nb = zs_ref.shape[0]
    n = pl.program_id(0)
    tiles_per_seq = pl.num_programs(0) // blocks_per_tile

    @pl.when(n == 0)
    def _():
        for b in range(nb):
            state_scr[b] = s0_ref[...]

    bd_mask = bd_ref[...]
    bd_f32 = bd_mask.astype(F32)
    tri = tri_ref[...]
    tile_start = n % blocks_per_tile == 0
    first_row = lax.broadcasted_iota(jnp.int32, (8, RWKV_IN), 0) == 0
    ti = lax.broadcasted_iota(jnp.int32, (CHUNK, GROUP_W), 0)
    si = lax.broadcasted_iota(jnp.int32, (CHUNK, GROUP_W), 1) % CHUNK
    strict = ti > si
    incl = ti >= si
    eye = (ti == si).astype(F32)
    n_chunks = BLOCK // CHUNK
    groups = range(N_GROUPS)
    chains = [(c, gi) for c in range(n_chunks) for gi in groups]

    def batch_stages(b):
        zs = zs_ref[b]
        tile = b * tiles_per_seq + n // blocks_per_tile
        z0 = zfirst_ref[tile]
        z0_prev = jnp.where(n == 0, zprev0_ref[...], zlast_ref[jnp.maximum(tile - 1, 0)])
        row0 = jnp.where(tile_start, z0 + (z0_prev - z0) * mu_ref[...], zs[0:1, :])
        zs = jnp.concatenate([jnp.where(first_row, row0, zs[0:8, :]), zs[8:, :]], axis=0)
        r = zs[:, :O_K]
        k = zs[:, O_K:O_V]
        v = zs[:, O_V:O_W]
        ld = -DECAY_SCALE * _sigmoid(w0_ref[...] + _mm(jnp.tanh(zs[:, O_W:O_A]), w2_ref[...]))
        a = _sigmoid(a0_ref[...] + _mm(zs[:, O_A:O_G], a2_ref[...]))
        g = _mm(_sigmoid(zs[:, O_G:]), g2_ref[...])
        kk = k * kk_ref[...]
        kk_sq = _head_sum(kk * kk, bd_mask)
        yield
        kk = kk * lax.rsqrt(jnp.maximum(kk_sq, 1e-24))
        k = k * (1.0 + (a - 1.0) * ka_ref[...])
        cum = _chunk_cumsum(tri, ld)
        yield
        e_pos = jnp.exp(cum)
        e_neg = jnp.exp(-cum)
        r_t = r * e_pos
        k_t = k * e_neg
        a_t = -kk * jnp.exp(cum - ld)
        b_t = kk * a * e_neg

        def part(x, ch):
            c, gi = ch
            return x[c * CHUNK:(c + 1) * CHUNK, gi * GROUP_W:(gi + 1) * GROUP_W]

        ar, m_ab, m_ak, m_rb, m_rk = {}, {}, {}, {}, {}
        for ch in chains:
            ar[ch] = jnp.concatenate([part(a_t, ch), part(r_t, ch)], axis=0).astype(BF16)
            bd_bk = jnp.concatenate([_block_diag(part(b_t, ch), bd_mask),
                                     _block_diag(part(k_t, ch), bd_mask)], axis=0)
            sc = _mm_nt(ar[ch], bd_bk)
            m_ab[ch] = jnp.where(strict, sc[:CHUNK, :GROUP_W], 0.0)
            m_ak[ch] = jnp.where(strict, sc[:CHUNK, GROUP_W:], 0.0)
            m_rb[ch] = jnp.where(incl, sc[CHUNK:, :GROUP_W], 0.0)
            m_rk[ch] = jnp.where(incl, sc[CHUNK:, GROUP_W:], 0.0)
        yield
        inv = {ch: eye + m_ab[ch] for ch in chains}
        pw = {ch: _mm(m_ab[ch], _block_diag(m_ab[ch], bd_mask)) for ch in chains}
        yield
        for _ in range(INV_DOUBLINGS):
            for ch in chains:
                both = _mm(jnp.concatenate([pw[ch], inv[ch]], axis=0), _block_diag(pw[ch], bd_mask))
                pw[ch] = both[:CHUNK]
                inv[ch] = inv[ch] + both[CHUNK:]
            yield
        for ch in chains:
            inv[ch] = inv[ch] + _mm(inv[ch], _block_diag(pw[ch], bd_mask))
        mv = {ch: _mm(jnp.concatenate([m_ak[ch], m_rk[ch]], axis=0),
                      _block_diag(part(v, ch), bd_mask)) for ch in chains}
        yield
        state = {gi: [state_scr[(b, gi) + tile] for tile in DIAG_TILES] for gi in groups}
        for c in range(n_chunks):
            cur = [(c, gi) for gi in groups]
            xr0 = {ch: _mm_nt(ar[ch], _from_diag_tiles([t.astype(BF16) for t in state[ch[1]]]))
                   for ch in cur}
            yield
            u = {ch: _mm(inv[ch], _block_diag(xr0[ch][:CHUNK] + mv[ch][:CHUNK], bd_mask))
                 for ch in cur}
            yield
            for ch in cur:
                gi = ch[1]
                y = xr0[ch][CHUNK:] + mv[ch][CHUNK:] + _mm(m_rb[ch], _block_diag(u[ch], bd_mask))
                upd = _mm_tn(jnp.concatenate([u[ch], part(v, ch)], axis=0),
                             jnp.concatenate([part(b_t, ch), part(k_t, ch)], axis=0))
                last = (c + 1) * CHUNK - 1
                w_end = e_pos[last:last + 1, gi * GROUP_W:(gi + 1) * GROUP_W]
                state[gi] = [(s + upd[tile] * bd_f32[tile]) * w_end[:, tile[1]]
                             for s, tile in zip(state[gi], DIAG_TILES)]
                ybuf[b, c * CHUNK:(c + 1) * CHUNK, gi * GROUP_W:(gi + 1) * GROUP_W] = y
            yield
        for gi in groups:
            for s, tile in zip(state[gi], DIAG_TILES):
                state_scr[(b, gi) + tile] = s
            if b == 0:
                sout_ref[gi] = _from_diag_tiles(state[gi])
        y = ybuf[b]
        mean = _head_sum(y, bd_mask) * (1.0 / RWKV_HEAD)
        bonus = _head_sum(r * k * rk_ref[...], bd_mask) * v
        yield
        yc = y - mean
        var = _head_sum(yc * yc, bd_mask) * (1.0 / RWKV_HEAD)
        yield
        yn = yc * lax.rsqrt(var + GN_EPS) * lnw_ref[...] + lnb_ref[...]
        y_ref[b] = ((yn + bonus) * g).astype(BF16)

    stages = [batch_stages(b) for b in range(nb)]
    live = set(range(nb))
    tick = 0
    while live:
        for b in sorted(live):
            if tick >= b * STAGE_SKEW and next(stages[b], "done") == "done":
                live.discard(b)
        tick += 1


def _rwkv(shifted, batch, zprev0, s0, tri, bd_mask, params):
    zs, zfirst, zlast, blocks_per_tile = shifted
    zs = zs.reshape(batch, -1, RWKV_IN)
    rows = zs.shape[1]
    state_shape = (N_GROUPS, GROUP_W, GROUP_W)
    return pl.pallas_call(
        functools.partial(_rwkv_kernel, blocks_per_tile=blocks_per_tile),
        grid=(rows // BLOCK,),
        in_specs=[
            pl.BlockSpec((batch, BLOCK, RWKV_IN), lambda n: (0, n, 0)),
            _const_spec(zfirst.shape),
            _const_spec(zlast.shape),
            _const_spec((1, RWKV_IN)),
            _const_spec(state_shape),
            _const_spec((BLOCK, BLOCK)),
            _const_spec((GROUP_W, GROUP_W)),
        ] + [_const_spec(p.shape) for p in params],
        out_specs=[
            pl.BlockSpec((batch, BLOCK, RWKV_WIDTH), lambda n: (0, n, 0)),
            pl.BlockSpec(state_shape, lambda n: (0, 0, 0)),
        ],
        out_shape=[
            jax.ShapeDtypeStruct((batch, rows, RWKV_WIDTH), BF16),
            jax.ShapeDtypeStruct(state_shape, F32),
        ],
        scratch_shapes=[
            pltpu.VMEM((batch,) + state_shape, F32),
            pltpu.VMEM((batch, BLOCK, RWKV_WIDTH), F32),
        ],
        compiler_params=pltpu.CompilerParams(
            dimension_semantics=("arbitrary",), vmem_limit_bytes=VMEM_LIMIT),
        name="rwkv",
    )(zs, zfirst, zlast, zprev0, s0, tri, bd_mask, *params)


def kernel(x, meta_tokens, ffn1_norm_pre, ffn1_w_gate_up, ffn1_w_down, ffn1_norm_post, mix_norm_pre, w_in, att_sinks, rwkv_mu, rwkv_w0, rwkv_w2, rwkv_a0, rwkv_a2, rwkv_g2, rwkv_k_k, rwkv_k_a, rwkv_r_k, rwkv_ln_w, rwkv_ln_b, w_att_branch, w_rwkv_branch, w_mix_out, mix_norm_post, ffn2_norm_pre, ffn2_w_gate_up, ffn2_w_down, ffn2_norm_post):
    batch, seq, _ = x.shape
    nblk = seq // BLOCK
    depth = w_in.shape[0]
    assert depth == 1 and seq % BLOCK == 0

    rope = _rope_consts()
    ri = jnp.arange(BLOCK)
    tri = ((ri[:, None] >= ri[None, :]) &
           (ri[:, None] // CHUNK == ri[None, :] // CHUNK)).astype(BF16)
    li = jnp.arange(GROUP_W) // RWKV_HEAD
    bd_mask = (li[:, None] == li[None, :]).astype(BF16)

    hx = x.reshape(batch * seq, D_MODEL)
    hm = jnp.concatenate([jnp.zeros((META_PAD, D_MODEL), x.dtype),
                          meta_tokens.astype(x.dtype)], axis=0)

    def row(p):
        return p.reshape(1, -1)

    l = 0
    wgu1, wd1 = ffn1_w_gate_up[l].astype(BF16), ffn1_w_down[l].astype(BF16)
    wgu2, wd2 = ffn2_w_gate_up[l].astype(BF16), ffn2_w_down[l].astype(BF16)
    w_in_l = w_in[l].astype(BF16)
    wa, wr, wo = (w_att_branch[l].astype(BF16), w_rwkv_branch[l].astype(BF16),
                  w_mix_out[l].astype(BF16))
    rw_params = (row(rwkv_mu[l]), row(rwkv_w0[l]), rwkv_w2[l], row(rwkv_a0[l]), rwkv_a2[l],
                 rwkv_g2[l], row(rwkv_k_k[l]), row(rwkv_k_a[l]), row(rwkv_r_k[l]),
                 row(rwkv_ln_w[l]), row(rwkv_ln_b[l]))

    outs = []
    for h, pos_base, seq_rows in ((hm, -META_PAD, BLOCK), (hx, N_META, seq)):
        h1 = _ffn(h, row(ffn1_norm_pre[l]), wgu1, wd1, row(ffn1_norm_post[l]))
        outs.append((h1,) + _inproj(h1, row(mix_norm_pre[l]), w_in_l, rope, row(rwkv_mu[l]),
                                    pos_base, seq_rows))
    (_, _, km, vm, shifted_m, _, _), (hx1, q, k, v, shifted_x, ga, gr) = outs

    o_att = _attention(att_sinks[l], q, k, v, km, vm, batch, nblk)

    zero_state = jnp.zeros((N_GROUPS, GROUP_W, GROUP_W), F32)
    _, s_meta = _rwkv(shifted_m, 1, jnp.zeros((1, RWKV_IN), F32), zero_state, tri, bd_mask,
                      rw_params)
    y_rwkv, _ = _rwkv(shifted_x, batch, shifted_m[2][0], s_meta, tri, bd_mask, rw_params)
    y_rwkv = y_rwkv.reshape(batch * seq, RWKV_WIDTH)

    hx = _ffn(hx1, row(ffn2_norm_pre[l]), wgu2, wd2, row(ffn2_norm_post[l]),
              merge=(o_att, y_rwkv, ga, gr, wa, wr, wo, row(mix_norm_post[l])))
    return hx.reshape(batch, seq, D_MODEL)
```

```python
import functools

import jax
import jax.numpy as jnp
import numpy as np
from jax import lax
from jax.experimental import pallas as pl
from jax.experimental.pallas import tpu as pltpu

D_MODEL = 1024
N_META = 16
ATT_HEADS = 8
ATT_KV_HEADS = 2
ATT_GROUP = ATT_HEADS // ATT_KV_HEADS
HEAD_DIM = 64
WINDOW = 128
BLOCK = 128
ROPE_THETA = 500000.0
ROT_DIM = HEAD_DIM // 4
ATT_Q = ATT_HEADS * HEAD_DIM
ATT_KV = ATT_KV_HEADS * HEAD_DIM
MASK_VALUE = -1e30
RWKV_HEADS = 8
RWKV_HEAD = 64
RWKV_WIDTH = RWKV_HEADS * RWKV_HEAD
DECAY_LORA = 64
AAA_LORA = 64
GATE_LORA = 128
GN_EPS = 64e-5
RWKV_IN = 3 * RWKV_WIDTH + DECAY_LORA + AAA_LORA + GATE_LORA
D_FF = 2816
NORM_EPS = 1e-6

META_PAD = BLOCK - N_META
CHUNK = 64
DECAY_SCALE = 0.6065306597126334
LOG2_E = 1.4426950408889634

LANES, SUBLANES = 128, 8
V7X_VMEM_BYTES = 64 * 1024 * 1024
VMEM_LIMIT = V7X_VMEM_BYTES - 8 * 1024 * 1024
ROW_TILE = 512
SUB_TILES = 2

F32 = jnp.float32
BF16 = jnp.bfloat16


def _rms(x, gain):
    ms = jnp.mean(x * x, axis=-1, keepdims=True)
    return x * lax.rsqrt(ms + NORM_EPS) * gain


def _dot(a, b):
    return jnp.dot(a, b, preferred_element_type=F32)


def _const_spec(shape):
    nd = len(shape)
    return pl.BlockSpec(shape, lambda *_: (0,) * nd, pipeline_mode=pl.Buffered(1))


def _sub_tiles(rows):
    sub = rows // SUB_TILES
    return [slice(i * sub, (i + 1) * sub) for i in range(SUB_TILES)]


def _ffn_kernel(*refs, with_merge):
    if with_merge:
        (h_ref, oa_ref, yr_ref, ga_ref, gr_ref, wa_ref, wr_ref, wo_ref, gmix_ref,
         gpre_ref, wgu_ref, wd_ref, gpost_ref, o_ref) = refs
    else:
        h_ref, gpre_ref, wgu_ref, wd_ref, gpost_ref, o_ref = refs
    parts = _sub_tiles(h_ref.shape[0])
    if with_merge:
        merged = []
        for p in parts:
            y_att = _dot(oa_ref[p, :], wa_ref[...])
            y_rwkv = _dot(yr_ref[p, :], wr_ref[...])
            merged.append((ga_ref[p, :].astype(F32) * y_att
                           + gr_ref[p, :].astype(F32) * y_rwkv).astype(BF16))
        hs = [h_ref[p, :] + _rms(_dot(m, wo_ref[...]), gmix_ref[...])
              for p, m in zip(parts, merged)]
    else:
        hs = [h_ref[p, :] for p in parts]
    acts = []
    for h in hs:
        hn = _rms(h, gpre_ref[...]).astype(BF16)
        gate = _dot(hn, wgu_ref[:, :D_FF])
        up = _dot(hn, wgu_ref[:, D_FF:])
        acts.append((gate * jax.nn.sigmoid(gate) * up).astype(BF16))
    for p, h, act in zip(parts, hs, acts):
        f = _dot(act, wd_ref[...])
        o_ref[p, :] = h + 0.5 * _rms(f, gpost_ref[...])


def _ffn(h, gpre, wgu, wd, gpost, merge=None):
    rows = h.shape[0]
    tm = min(ROW_TILE, rows)

    def row(i):
        return (i, 0)

    operands = [h]
    in_specs = [pl.BlockSpec((tm, D_MODEL), row)]
    if merge is not None:
        o_att, y_rwkv, ga, gr, wa, wr, wo, gmix = merge
        operands += [o_att, y_rwkv, ga, gr, wa, wr, wo, gmix]
        in_specs += [
            pl.BlockSpec((tm, ATT_Q), row),
            pl.BlockSpec((tm, RWKV_WIDTH), row),
            pl.BlockSpec((tm, D_MODEL), row),
            pl.BlockSpec((tm, D_MODEL), row),
            _const_spec((ATT_Q, D_MODEL)),
            _const_spec((RWKV_WIDTH, D_MODEL)),
            _const_spec((D_MODEL, D_MODEL)),
            _const_spec((1, D_MODEL)),
        ]
    operands += [gpre, wgu, wd, gpost]
    in_specs += [
        _const_spec((1, D_MODEL)),
        _const_spec((D_MODEL, 2 * D_FF)),
        _const_spec((D_FF, D_MODEL)),
        _const_spec((1, D_MODEL)),
    ]
    return pl.pallas_call(
        functools.partial(_ffn_kernel, with_merge=merge is not None),
        grid=(rows // tm,),
        in_specs=in_specs,
        out_specs=pl.BlockSpec((tm, D_MODEL), row),
        out_shape=jax.ShapeDtypeStruct((rows, D_MODEL), F32),
        compiler_params=pltpu.CompilerParams(
            dimension_semantics=("parallel",), vmem_limit_bytes=VMEM_LIMIT),
        name="merge_ffn" if merge is not None else "ffn",
    )(*operands)


C_Q = ATT_Q
C_K = C_Q + ATT_KV
C_V = C_K + ATT_KV
C_R = C_V + RWKV_IN
C_GA = C_R + D_MODEL
IN_COLS = C_GA + D_MODEL


def _rope(x, cos_t, sin_lo, sin_hi):
    n = x.shape[1] // LANES
    if n > 1:
        cos_t = jnp.concatenate([cos_t] * n, axis=1)
        sin_lo = jnp.concatenate([sin_lo] * n, axis=1)
        sin_hi = jnp.concatenate([sin_hi] * n, axis=1)
    width = x.shape[1]
    half = ROT_DIM // 2
    from_hi = pltpu.roll(x, width - half, 1)
    from_lo = pltpu.roll(x, half, 1)
    return x * cos_t + from_hi * sin_lo + from_lo * sin_hi


def _inproj_kernel(h_ref, g_ref, w_ref, rope_ref, cosr_ref, sinr_ref, mu_ref,
                   q_ref, k_ref, v_ref, zs_ref, zfirst_ref, zlast_ref, ga_ref, gr_ref,
                   *, pos_base, tiles_per_seq):
    tm = h_ref.shape[0]
    tile_pos = pos_base + (pl.program_id(0) % tiles_per_seq) * tm
    freq, sign_lo, sign_hi = rope_ref[0:1, :], rope_ref[1:2, :], rope_ref[2:3, :]
    carry = None
    for p in _sub_tiles(tm):
        rows = p.stop - p.start
        u = _rms(h_ref[p, :], g_ref[...]).astype(BF16)
        z = _dot(u, w_ref[:, C_V:C_R])
        z_prev = pltpu.roll(z, 1, 0)
        if carry is None:
            zfirst_ref[...] = z[0:1, :]
        else:
            row_id = lax.broadcasted_iota(jnp.int32, z.shape, 0)
            z_prev = jnp.where(row_id == 0, carry, z_prev)
        zs_ref[p, :] = z + (z_prev - z) * mu_ref[...]
        carry = z[rows - 1:rows, :]
        ga_ref[p, :] = jax.nn.sigmoid(_dot(u, w_ref[:, C_R:C_GA])).astype(BF16)
        gr_ref[p, :] = jax.nn.sigmoid(_dot(u, w_ref[:, C_GA:])).astype(BF16)
        v_ref[p, :] = _dot(u, w_ref[:, C_K:C_V]).astype(BF16)
        base = jnp.full((1, LANES), tile_pos + p.start, jnp.int32).astype(F32) * freq
        cos_b, sin_b = jnp.cos(base), jnp.sin(base)
        cos_r, sin_r = cosr_ref[...], sinr_ref[...]
        cos_t = cos_b * cos_r - sin_b * sin_r
        sin = sin_b * cos_r + cos_b * sin_r
        sin_lo, sin_hi = sin * sign_lo, sin * sign_hi
        q = _dot(u, w_ref[:, :C_Q])
        q_ref[p, :] = (_rope(q, cos_t, sin_lo, sin_hi) * (LOG2_E * HEAD_DIM ** -0.5)).astype(BF16)
        k = _dot(u, w_ref[:, C_Q:C_K])
        k_ref[p, :] = _rope(k, cos_t, sin_lo, sin_hi).astype(BF16)
    zlast_ref[...] = carry


def _inproj(h, gain, w_in, rope_consts, mu, pos_base, seq_rows):
    rope, cos_r, sin_r = rope_consts
    rows = h.shape[0]
    tm = min(ROW_TILE, rows)
    ntiles = rows // tm
    sub_rows = tm // SUB_TILES

    def row(i):
        return (i, 0)

    widths = (ATT_Q, ATT_KV, ATT_KV, RWKV_IN, D_MODEL, D_MODEL)
    dtypes = (BF16, BF16, BF16, F32, BF16, BF16)
    edge_spec = pl.BlockSpec((None, 1, RWKV_IN), lambda i: (i, 0, 0))
    edge_shape = jax.ShapeDtypeStruct((ntiles, 1, RWKV_IN), F32)
    out_specs = [pl.BlockSpec((tm, w), row) for w in widths]
    out_shape = [jax.ShapeDtypeStruct((rows, w), d) for w, d in zip(widths, dtypes)]
    q, k, v, zs, zfirst, zlast, ga, gr = pl.pallas_call(
        functools.partial(_inproj_kernel, pos_base=pos_base, tiles_per_seq=seq_rows // tm),
        grid=(ntiles,),
        in_specs=[
            pl.BlockSpec((tm, D_MODEL), row),
            _const_spec((1, D_MODEL)),
            _const_spec((D_MODEL, IN_COLS)),
            _const_spec((SUBLANES, LANES)),
            _const_spec((sub_rows, LANES)),
            _const_spec((sub_rows, LANES)),
            _const_spec((1, RWKV_IN)),
        ],
        out_specs=out_specs[:4] + [edge_spec, edge_spec] + out_specs[4:],
        out_shape=out_shape[:4] + [edge_shape, edge_shape] + out_shape[4:],
        compiler_params=pltpu.CompilerParams(
            dimension_semantics=("parallel",), vmem_limit_bytes=VMEM_LIMIT),
        name="inproj",
    )(h, gain, w_in, rope, cos_r[:sub_rows], sin_r[:sub_rows], mu)
    return q, k, v, (zs, zfirst, zlast, tm // BLOCK), ga, gr


def _rope_consts():
    half = ROT_DIM // 2
    inv_freq = 1.0 / (ROPE_THETA ** (jnp.arange(half, dtype=F32) * (2.0 / ROT_DIM)))
    lane = np.arange(LANES) % HEAD_DIM
    freq = jnp.where(lane < ROT_DIM, inv_freq[lane % half], 0.0)
    sign_lo = jnp.asarray(np.where(lane < half, -1.0, 0.0), F32)
    sign_hi = jnp.asarray(np.where((lane >= half) & (lane < ROT_DIM), 1.0, 0.0), F32)
    rope = jnp.concatenate([jnp.stack([freq, sign_lo, sign_hi]),
                            jnp.zeros((SUBLANES - 3, LANES), F32)], axis=0)
    offs = jnp.arange(ROW_TILE // SUB_TILES, dtype=F32)[:, None] * freq[None, :]
    return rope, jnp.cos(offs), jnp.sin(offs)


ATT_TILE_BLOCKS = 4


def _attn_kernel(sink_ref, q_ref, k_ref, v_ref, kp_ref, vp_ref, km_ref, vm_ref, o_ref):
    first = pl.program_id(1) == 0
    k_prev = jnp.where(first, km_ref[...], kp_ref[...])
    v_prev = jnp.where(first, vm_ref[...], vp_ref[...])
    k_all = jnp.concatenate([k_prev, k_ref[...]], axis=0)
    v_all = jnp.concatenate([v_prev, v_ref[...]], axis=0)
    row = lax.broadcasted_iota(jnp.int32, (BLOCK, 2 * BLOCK), 0)
    col = lax.broadcasted_iota(jnp.int32, (BLOCK, 2 * BLOCK), 1)
    mask = (col > row) & (col <= row + WINDOW)
    mask_first = (col > jnp.where(first, jnp.maximum(row, META_PAD - 1), row)) & (col <= row + WINDOW)
    sink_col = col == 0
    sink_row = lax.broadcasted_iota(jnp.int32, (2 * BLOCK, ATT_KV), 0) == 0

    pairs = [(j, g) for j in range(ATT_TILE_BLOCKS) for g in range(ATT_KV_HEADS)]
    scores = {}
    for j, g in pairs:
        qj = q_ref[j * BLOCK:(j + 1) * BLOCK, :]
        kg = k_all[j * BLOCK:(j + 2) * BLOCK, g * HEAD_DIM:(g + 1) * HEAD_DIM]
        q4 = jnp.concatenate([qj[:, hd * HEAD_DIM:(hd + 1) * HEAD_DIM]
                              for hd in range(g * ATT_GROUP, (g + 1) * ATT_GROUP)], axis=0)
        scores[j, g] = lax.dot_general(q4, kg, (((1,), (1,)), ((), ())),
                                       preferred_element_type=F32)
    logits, maxima = {}, {}
    for j, g in pairs:
        mj = mask_first if j == 0 else mask
        for i in range(ATT_GROUP):
            sh = jnp.where(mj, scores[j, g][i * BLOCK:(i + 1) * BLOCK], MASK_VALUE)
            sh = jnp.where(sink_col, sink_ref[g * ATT_GROUP + i] * LOG2_E, sh)
            logits[j, g, i] = sh
            maxima[j, g, i] = jnp.max(sh, axis=-1, keepdims=True)
    probs, denoms = {}, {}
    for key, sh in logits.items():
        p = jnp.exp2(sh - maxima[key])
        denoms[key] = jnp.sum(p, axis=-1, keepdims=True)
        probs[key] = p.astype(BF16)
    for j in range(ATT_TILE_BLOCKS):
        vj = jnp.where(sink_row, 0.0, v_all[j * BLOCK:(j + 2) * BLOCK].astype(F32)).astype(BF16)
        outs = []
        for g in range(ATT_KV_HEADS):
            p4 = jnp.concatenate([probs[j, g, i] for i in range(ATT_GROUP)], axis=0)
            o4 = _dot(p4, vj[:, g * HEAD_DIM:(g + 1) * HEAD_DIM])
            outs += [o4[i * BLOCK:(i + 1) * BLOCK] / denoms[j, g, i] for i in range(ATT_GROUP)]
        o_ref[j * BLOCK:(j + 1) * BLOCK, :] = jnp.concatenate(outs, axis=1).astype(BF16)


def _attention(sinks, q, k, v, k_meta, v_meta, batch, nblk):
    tile = ATT_TILE_BLOCKS * BLOCK
    ntile = nblk // ATT_TILE_BLOCKS

    def own(b, n):
        return (b * ntile + n, 0)

    def prev(b, n):
        return (b * nblk + jnp.maximum(n * ATT_TILE_BLOCKS - 1, 0), 0)

    return pl.pallas_call(
        _attn_kernel,
        grid=(batch, ntile),
        in_specs=[
            pl.BlockSpec(memory_space=pltpu.SMEM),
            pl.BlockSpec((tile, ATT_Q), own),
            pl.BlockSpec((tile, ATT_KV), own),
            pl.BlockSpec((tile, ATT_KV), own),
            pl.BlockSpec((BLOCK, ATT_KV), prev),
            pl.BlockSpec((BLOCK, ATT_KV), prev),
            _const_spec((BLOCK, ATT_KV)),
            _const_spec((BLOCK, ATT_KV)),
        ],
        out_specs=pl.BlockSpec((tile, ATT_Q), own),
        out_shape=jax.ShapeDtypeStruct((batch * nblk * BLOCK, ATT_Q), BF16),
        compiler_params=pltpu.CompilerParams(
            dimension_semantics=("parallel", "parallel"), vmem_limit_bytes=VMEM_LIMIT),
        name="attention",
    )(sinks, q, k, v, k, v, k_meta, v_meta)


O_K = RWKV_WIDTH
O_V = 2 * RWKV_WIDTH
O_W = 3 * RWKV_WIDTH
O_A = O_W + DECAY_LORA
O_G = O_A + AAA_LORA

GROUP_HEADS = 4
GROUP_W = GROUP_HEADS * RWKV_HEAD
N_GROUPS = RWKV_HEADS // GROUP_HEADS
CUMSUM_PARTS = 2
INV_BASE = 8
STAGE_SKEW = 1


def _mm(a, b):
    return jnp.dot(a.astype(BF16), b.astype(BF16), preferred_element_type=F32)


def _mm_nt(a, b):
    return lax.dot_general(a.astype(BF16), b.astype(BF16), (((1,), (1,)), ((), ())),
                           preferred_element_type=F32)


def _mm_tn(a, b):
    return lax.dot_general(a, b, (((0,), (0,)), ((), ())), preferred_element_type=F32)


def _sigmoid(x):
    return 0.5 * jnp.tanh(0.5 * x) + 0.5


DIAG_TILES = [(slice(h * RWKV_HEAD, (h + 1) * RWKV_HEAD),
               slice((h // 2) * LANES, (h // 2 + 1) * LANES)) for h in range(GROUP_HEADS)]


def _diag_tiles(x, bd_mask):
    return [x[:, lanes] * bd_mask[rows, lanes] for rows, lanes in DIAG_TILES]


def _from_diag_tiles(tiles):
    zero = jnp.zeros_like(tiles[0])
    return jnp.concatenate(
        [jnp.concatenate([t, zero] if h // 2 == 0 else [zero, t], axis=1)
         for h, t in enumerate(tiles)], axis=0)


def _block_diag(x, bd_mask):
    return _from_diag_tiles(_diag_tiles(x.astype(BF16), bd_mask))


def _head_sum(x, bd_mask):
    return jnp.concatenate(
        [_mm(x[:, i * GROUP_W:(i + 1) * GROUP_W], bd_mask) for i in range(N_GROUPS)], axis=1)


def _chunk_cumsum(tri, x):
    pieces, rem = [], x
    for i in range(CUMSUM_PARTS):
        hi = rem.astype(BF16)
        pieces.append(hi)
        if i + 1 < CUMSUM_PARTS:
            rem = rem - hi.astype(F32)
    out = jnp.dot(tri, jnp.concatenate(pieces, axis=1), preferred_element_type=F32)
    w = x.shape[1]
    return sum(out[:, i * w:(i + 1) * w] for i in range(CUMSUM_PARTS))


def _rwkv_kernel(zs_ref, zfirst_ref, zlast_ref, zprev0_ref, s0_ref, tri_ref, bd_ref, mu_ref, w0_ref,
                 w2_ref, a0_ref, a2_ref, g2_ref, kk_ref, ka_ref, rk_ref, lnw_ref, lnb_ref,
                 y_ref, sout_ref, state_scr, ybuf, *, blocks_per_tile):
    nb = zs_ref.shape[0]
    n = pl.program_id(0)
    tiles_per_seq = pl.num_programs(0) // blocks_per_tile

    @pl.when(n == 0)
    def _():
        for b in range(nb):
            state_scr[b] = s0_ref[...]

    bd_mask = bd_ref[...]
    bd_f32 = bd_mask.astype(F32)
    tri = tri_ref[...]
    tile_start = n % blocks_per_tile == 0
    first_row = lax.broadcasted_iota(jnp.int32, (SUBLANES, RWKV_IN), 0) == 0
    ti = lax.broadcasted_iota(jnp.int32, (CHUNK, GROUP_W), 0)
    si = lax.broadcasted_iota(jnp.int32, (CHUNK, GROUP_W), 1) % CHUNK
    strict = ti > si
    incl = ti >= si
    eye = (ti == si).astype(F32)

    def same_block(size):
        return (ti // size) == (si // size)

    base_mask = strict & same_block(INV_BASE)
    merge_masks = []
    size = INV_BASE
    while size < CHUNK:
        merge_masks.append(strict & same_block(2 * size) & jnp.logical_not(same_block(size)))
        size *= 2
    n_chunks = BLOCK // CHUNK
    groups = range(N_GROUPS)
    chains = [(c, gi) for c in range(n_chunks) for gi in groups]

    def batch_stages(b):
        zs = zs_ref[b]
        tile = b * tiles_per_seq + n // blocks_per_tile
        z0 = zfirst_ref[tile]
        z0_prev = jnp.where(n == 0, zprev0_ref[...], zlast_ref[jnp.maximum(tile - 1, 0)])
        row0 = jnp.where(tile_start, z0 + (z0_prev - z0) * mu_ref[...], zs[0:1, :])
        zs = jnp.concatenate([jnp.where(first_row, row0, zs[:SUBLANES, :]), zs[SUBLANES:, :]], axis=0)
        r = zs[:, :O_K]
        k = zs[:, O_K:O_V]
        v = zs[:, O_V:O_W]
        ld = -DECAY_SCALE * _sigmoid(w0_ref[...] + _mm(jnp.tanh(zs[:, O_W:O_A]), w2_ref[...]))
        a = _sigmoid(a0_ref[...] + _mm(zs[:, O_A:O_G], a2_ref[...]))
        g = _mm(_sigmoid(zs[:, O_G:]), g2_ref[...])
        kk = k * kk_ref[...]
        kk_sq = _head_sum(kk * kk, bd_mask)
        yield
        kk = kk * lax.rsqrt(jnp.maximum(kk_sq, 1e-24))
        k = k * (1.0 + (a - 1.0) * ka_ref[...])
        cum = _chunk_cumsum(tri, ld)
        yield
        e_pos = jnp.exp(cum)
        e_neg = jnp.exp(-cum)
        r_t = r * e_pos
        k_t = k * e_neg
        a_t = -kk * jnp.exp(cum - ld)
        b_t = kk * a * e_neg

        def part(x, ch):
            c, gi = ch
            return x[c * CHUNK:(c + 1) * CHUNK, gi * GROUP_W:(gi + 1) * GROUP_W]

        ar, m_ab, m_ak, m_rb, m_rk = {}, {}, {}, {}, {}
        for ch in chains:
            ar[ch] = jnp.concatenate([part(a_t, ch), part(r_t, ch)], axis=0).astype(BF16)
            bd_bk = jnp.concatenate([_block_diag(part(b_t, ch), bd_mask),
                                     _block_diag(part(k_t, ch), bd_mask)], axis=0)
            sc = _mm_nt(ar[ch], bd_bk)
            m_ab[ch] = sc[:CHUNK, :GROUP_W]
            m_ak[ch] = jnp.where(strict, sc[:CHUNK, GROUP_W:], 0.0)
            m_rb[ch] = jnp.where(incl, sc[CHUNK:, :GROUP_W], 0.0)
            m_rk[ch] = jnp.where(incl, sc[CHUNK:, GROUP_W:], 0.0)
        yield
        low = {ch: jnp.where(base_mask, m_ab[ch], 0.0) for ch in chains}
        pw = {ch: _mm(low[ch], _block_diag(low[ch], bd_mask)) for ch in chains}
        yield
        inv = {ch: eye + low[ch] for ch in chains}
        for _ in range(INV_BASE.bit_length() - 3):
            for ch in chains:
                both = _mm(jnp.concatenate([pw[ch], inv[ch]], axis=0), _block_diag(pw[ch], bd_mask))
                pw[ch] = both[:CHUNK]
                inv[ch] = inv[ch] + both[CHUNK:]
            yield
        for ch in chains:
            inv[ch] = inv[ch] + _mm(inv[ch], _block_diag(pw[ch], bd_mask))
        yield
        for mask in merge_masks:
            join = {ch: _mm(jnp.where(mask, m_ab[ch], 0.0), _block_diag(inv[ch], bd_mask))
                    for ch in chains}
            yield
            for ch in chains:
                inv[ch] = inv[ch] + _mm(inv[ch], _block_diag(join[ch], bd_mask))
            yield
        mv = {ch: _mm(jnp.concatenate([m_ak[ch], m_rk[ch]], axis=0),
                      _block_diag(part(v, ch), bd_mask)) for ch in chains}
        yield
        state = {gi: [state_scr[(b, gi) + tile] for tile in DIAG_TILES] for gi in groups}
        for c in range(n_chunks):
            cur = [(c, gi) for gi in groups]
            xr0 = {ch: _mm_nt(ar[ch], _from_diag_tiles([t.astype(BF16) for t in state[ch[1]]]))
                   for ch in cur}
            yield
            u = {ch: _mm(inv[ch], _block_diag(xr0[ch][:CHUNK] + mv[ch][:CHUNK], bd_mask))
                 for ch in cur}
            yield
            for ch in cur:
                gi = ch[1]
                y = xr0[ch][CHUNK:] + mv[ch][CHUNK:] + _mm(m_rb[ch], _block_diag(u[ch], bd_mask))
                upd = _mm_tn(jnp.concatenate([u[ch], part(v, ch)], axis=0),
                             jnp.concatenate([part(b_t, ch), part(k_t, ch)], axis=0))
                last = (c + 1) * CHUNK - 1
                w_end = e_pos[last:last + 1, gi * GROUP_W:(gi + 1) * GROUP_W]
                state[gi] = [(s + upd[tile] * bd_f32[tile]) * w_end[:, tile[1]]
                             for s, tile in zip(state[gi], DIAG_TILES)]
                ybuf[b, c * CHUNK:(c + 1) * CHUNK, gi * GROUP_W:(gi + 1) * GROUP_W] = y
            yield
        for gi in groups:
            for s, tile in zip(state[gi], DIAG_TILES):
                state_scr[(b, gi) + tile] = s
            if b == 0:
                sout_ref[gi] = _from_diag_tiles(state[gi])
        y = ybuf[b]
        mean = _head_sum(y, bd_mask) * (1.0 / RWKV_HEAD)
        bonus = _head_sum(r * k * rk_ref[...], bd_mask) * v
        yield
        yc = y - mean
        var = _head_sum(yc * yc, bd_mask) * (1.0 / RWKV_HEAD)
        yield
        yn = yc * lax.rsqrt(var + GN_EPS) * lnw_ref[...] + lnb_ref[...]
        y_ref[b] = ((yn + bonus) * g).astype(BF16)

    stages = [batch_stages(b) for b in range(nb)]
    live = set(range(nb))
    tick = 0
    while live:
        for b in sorted(live):
            if tick >= b * STAGE_SKEW and next(stages[b], "done") == "done":
                live.discard(b)
        tick += 1


def _rwkv(shifted, batch, zprev0, s0, tri, bd_mask, params):
    zs, zfirst, zlast, blocks_per_tile = shifted
    zs = zs.reshape(batch, -1, RWKV_IN)
    rows = zs.shape[1]
    state_shape = (N_GROUPS, GROUP_W, GROUP_W)
    return pl.pallas_call(
        functools.partial(_rwkv_kernel, blocks_per_tile=blocks_per_tile),
        grid=(rows // BLOCK,),
        in_specs=[
            pl.BlockSpec((batch, BLOCK, RWKV_IN), lambda n: (0, n, 0)),
            _const_spec(zfirst.shape),
            _const_spec(zlast.shape),
            _const_spec((1, RWKV_IN)),
            _const_spec(state_shape),
            _const_spec((BLOCK, BLOCK)),
            _const_spec((GROUP_W, GROUP_W)),
        ] + [_const_spec(p.shape) for p in params],
        out_specs=[
            pl.BlockSpec((batch, BLOCK, RWKV_WIDTH), lambda n: (0, n, 0)),
            pl.BlockSpec(state_shape, lambda n: (0, 0, 0)),
        ],
        out_shape=[
            jax.ShapeDtypeStruct((batch, rows, RWKV_WIDTH), BF16),
            jax.ShapeDtypeStruct(state_shape, F32),
        ],
        scratch_shapes=[
            pltpu.VMEM((batch,) + state_shape, F32),
            pltpu.VMEM((batch, BLOCK, RWKV_WIDTH), F32),
        ],
        compiler_params=pltpu.CompilerParams(
            dimension_semantics=("arbitrary",), vmem_limit_bytes=VMEM_LIMIT),
        name="rwkv",
    )(zs, zfirst, zlast, zprev0, s0, tri, bd_mask, *params)


def kernel(x, meta_tokens, ffn1_norm_pre, ffn1_w_gate_up, ffn1_w_down, ffn1_norm_post, mix_norm_pre, w_in, att_sinks, rwkv_mu, rwkv_w0, rwkv_w2, rwkv_a0, rwkv_a2, rwkv_g2, rwkv_k_k, rwkv_k_a, rwkv_r_k, rwkv_ln_w, rwkv_ln_b, w_att_branch, w_rwkv_branch, w_mix_out, mix_norm_post, ffn2_norm_pre, ffn2_w_gate_up, ffn2_w_down, ffn2_norm_post):
    batch, seq, _ = x.shape
    nblk = seq // BLOCK
    depth = w_in.shape[0]
    assert depth == 1 and seq % BLOCK == 0

    rope = _rope_consts()
    ri = jnp.arange(BLOCK)
    tri = ((ri[:, None] >= ri[None, :]) &
           (ri[:, None] // CHUNK == ri[None, :] // CHUNK)).astype(BF16)
    li = jnp.arange(GROUP_W) // RWKV_HEAD
    bd_mask = (li[:, None] == li[None, :]).astype(BF16)

    hx = x.reshape(batch * seq, D_MODEL)
    hm = jnp.concatenate([jnp.zeros((META_PAD, D_MODEL), x.dtype),
                          meta_tokens.astype(x.dtype)], axis=0)

    def row(p):
        return p.reshape(1, -1)

    l = 0
    wgu1, wd1 = ffn1_w_gate_up[l].astype(BF16), ffn1_w_down[l].astype(BF16)
    wgu2, wd2 = ffn2_w_gate_up[l].astype(BF16), ffn2_w_down[l].astype(BF16)
    w_in_l = w_in[l].astype(BF16)
    wa, wr, wo = (w_att_branch[l].astype(BF16), w_rwkv_branch[l].astype(BF16),
                  w_mix_out[l].astype(BF16))
    rw_params = (row(rwkv_mu[l]), row(rwkv_w0[l]), rwkv_w2[l], row(rwkv_a0[l]), rwkv_a2[l],
                 rwkv_g2[l], row(rwkv_k_k[l]), row(rwkv_k_a[l]), row(rwkv_r_k[l]),
                 row(rwkv_ln_w[l]), row(rwkv_ln_b[l]))

    outs = []
    for h, pos_base, seq_rows in ((hm, -META_PAD, BLOCK), (hx, N_META, seq)):
        h1 = _ffn(h, row(ffn1_norm_pre[l]), wgu1, wd1, row(ffn1_norm_post[l]))
        outs.append((h1,) + _inproj(h1, row(mix_norm_pre[l]), w_in_l, rope, row(rwkv_mu[l]),
                                    pos_base, seq_rows))
    (_, _, km, vm, shifted_m, _, _), (hx1, q, k, v, shifted_x, ga, gr) = outs

    o_att = _attention(att_sinks[l], q, k, v, km, vm, batch, nblk)

    zero_state = jnp.zeros((N_GROUPS, GROUP_W, GROUP_W), F32)
    _, s_meta = _rwkv(shifted_m, 1, jnp.zeros((1, RWKV_IN), F32), zero_state, tri, bd_mask,
                      rw_params)
    y_rwkv, _ = _rwkv(shifted_x, batch, shifted_m[2][0], s_meta, tri, bd_mask, rw_params)
    y_rwkv = y_rwkv.reshape(batch * seq, RWKV_WIDTH)

    hx = _ffn(hx1, row(ffn2_norm_pre[l]), wgu2, wd2, row(ffn2_norm_post[l]),
              merge=(o_att, y_rwkv, ga, gr, wa, wr, wo, row(mix_norm_post[l])))
    return hx.reshape(batch, seq, D_MODEL)
```

```python
import functools

import jax
import jax.numpy as jnp
import numpy as np
from jax import lax
from jax.experimental import pallas as pl
from jax.experimental.pallas import tpu as pltpu

D_MODEL = 1024
N_META = 16
ATT_HEADS = 8
ATT_KV_HEADS = 2
ATT_GROUP = ATT_HEADS // ATT_KV_HEADS
HEAD_DIM = 64
WINDOW = 128
BLOCK = 128
ROPE_THETA = 500000.0
ROT_DIM = HEAD_DIM // 4
ATT_Q = ATT_HEADS * HEAD_DIM
ATT_KV = ATT_KV_HEADS * HEAD_DIM
MASK_VALUE = -1e30
RWKV_HEADS = 8
RWKV_HEAD = 64
RWKV_WIDTH = RWKV_HEADS * RWKV_HEAD
DECAY_LORA = 64
AAA_LORA = 64
GATE_LORA = 128
GN_EPS = 64e-5
RWKV_IN = 3 * RWKV_WIDTH + DECAY_LORA + AAA_LORA + GATE_LORA
D_FF = 2816
NORM_EPS = 1e-6

META_PAD = BLOCK - N_META
CHUNK = 64
DECAY_SCALE = 0.6065306597126334
LOG2_E = 1.4426950408889634

LANES, SUBLANES = 128, 8
V7X_VMEM_BYTES = 64 * 1024 * 1024
VMEM_LIMIT = V7X_VMEM_BYTES - 8 * 1024 * 1024
ROW_TILE = 512
SUB_TILES = 2

F32 = jnp.float32
BF16 = jnp.bfloat16


def _rms(x, gain):
    ms = jnp.mean(x * x, axis=-1, keepdims=True)
    return x * lax.rsqrt(ms + NORM_EPS) * gain


def _dot(a, b):
    return jnp.dot(a, b, preferred_element_type=F32)


def _const_spec(shape):
    nd = len(shape)
    return pl.BlockSpec(shape, lambda *_: (0,) * nd, pipeline_mode=pl.Buffered(1))


def _sub_tiles(rows):
    sub = rows // SUB_TILES
    return [slice(i * sub, (i + 1) * sub) for i in range(SUB_TILES)]


def _ffn_kernel(*refs, with_merge):
    if with_merge:
        (h_ref, oa_ref, yr_ref, ga_ref, gr_ref, wa_ref, wr_ref, wo_ref, gmix_ref,
         gpre_ref, wgu_ref, wd_ref, gpost_ref, o_ref) = refs
    else:
        h_ref, gpre_ref, wgu_ref, wd_ref, gpost_ref, o_ref = refs
    parts = _sub_tiles(h_ref.shape[0])
    if with_merge:
        merged = []
        for p in parts:
            y_att = _dot(oa_ref[p, :], wa_ref[...])
            y_rwkv = _dot(yr_ref[p, :], wr_ref[...])
            merged.append((ga_ref[p, :].astype(F32) * y_att
                           + gr_ref[p, :].astype(F32) * y_rwkv).astype(BF16))
        hs = [h_ref[p, :] + _rms(_dot(m, wo_ref[...]), gmix_ref[...])
              for p, m in zip(parts, merged)]
    else:
        hs = [h_ref[p, :] for p in parts]
    acts = []
    for h in hs:
        hn = _rms(h, gpre_ref[...]).astype(BF16)
        gate = _dot(hn, wgu_ref[:, :D_FF])
        up = _dot(hn, wgu_ref[:, D_FF:])
        acts.append((gate * jax.nn.sigmoid(gate) * up).astype(BF16))
    for p, h, act in zip(parts, hs, acts):
        f = _dot(act, wd_ref[...])
        o_ref[p, :] = h + 0.5 * _rms(f, gpost_ref[...])


def _ffn(h, gpre, wgu, wd, gpost, merge=None):
    rows = h.shape[0]
    tm = min(ROW_TILE, rows)

    def row(i):
        return (i, 0)

    operands = [h]
    in_specs = [pl.BlockSpec((tm, D_MODEL), row)]
    if merge is not None:
        o_att, y_rwkv, ga, gr, wa, wr, wo, gmix = merge
        operands += [o_att, y_rwkv, ga, gr, wa, wr, wo, gmix]
        in_specs += [
            pl.BlockSpec((tm, ATT_Q), row),
            pl.BlockSpec((tm, RWKV_WIDTH), row),
            pl.BlockSpec((tm, D_MODEL), row),
            pl.BlockSpec((tm, D_MODEL), row),
            _const_spec((ATT_Q, D_MODEL)),
            _const_spec((RWKV_WIDTH, D_MODEL)),
            _const_spec((D_MODEL, D_MODEL)),
            _const_spec((1, D_MODEL)),
        ]
    operands += [gpre, wgu, wd, gpost]
    in_specs += [
        _const_spec((1, D_MODEL)),
        _const_spec((D_MODEL, 2 * D_FF)),
        _const_spec((D_FF, D_MODEL)),
        _const_spec((1, D_MODEL)),
    ]
    return pl.pallas_call(
        functools.partial(_ffn_kernel, with_merge=merge is not None),
        grid=(rows // tm,),
        in_specs=in_specs,
        out_specs=pl.BlockSpec((tm, D_MODEL), row),
        out_shape=jax.ShapeDtypeStruct((rows, D_MODEL), F32),
        compiler_params=pltpu.CompilerParams(
            dimension_semantics=("parallel",), vmem_limit_bytes=VMEM_LIMIT),
        name="merge_ffn" if merge is not None else "ffn",
    )(*operands)


C_Q = ATT_Q
C_K = C_Q + ATT_KV
C_V = C_K + ATT_KV
C_R = C_V + RWKV_IN
C_GA = C_R + D_MODEL
IN_COLS = C_GA + D_MODEL


def _rope(x, cos_t, sin_lo, sin_hi):
    n = x.shape[1] // LANES
    if n > 1:
        cos_t = jnp.concatenate([cos_t] * n, axis=1)
        sin_lo = jnp.concatenate([sin_lo] * n, axis=1)
        sin_hi = jnp.concatenate([sin_hi] * n, axis=1)
    width = x.shape[1]
    half = ROT_DIM // 2
    from_hi = pltpu.roll(x, width - half, 1)
    from_lo = pltpu.roll(x, half, 1)
    return x * cos_t + from_hi * sin_lo + from_lo * sin_hi


def _inproj_kernel(h_ref, g_ref, w_ref, rope_ref, cosr_ref, sinr_ref, mu_ref,
                   q_ref, k_ref, v_ref, zs_ref, zfirst_ref, zlast_ref, ga_ref, gr_ref,
                   *, pos_base, tiles_per_seq):
    tm = h_ref.shape[0]
    tile_pos = pos_base + (pl.program_id(0) % tiles_per_seq) * tm
    freq, sign_lo, sign_hi = rope_ref[0:1, :], rope_ref[1:2, :], rope_ref[2:3, :]
    carry = None
    for p in _sub_tiles(tm):
        rows = p.stop - p.start
        u = _rms(h_ref[p, :], g_ref[...]).astype(BF16)
        z = _dot(u, w_ref[:, C_V:C_R])
        z_prev = pltpu.roll(z, 1, 0)
        if carry is None:
            zfirst_ref[...] = z[0:1, :]
        else:
            row_id = lax.broadcasted_iota(jnp.int32, z.shape, 0)
            z_prev = jnp.where(row_id == 0, carry, z_prev)
        zs_ref[p, :] = z + (z_prev - z) * mu_ref[...]
        carry = z[rows - 1:rows, :]
        ga_ref[p, :] = jax.nn.sigmoid(_dot(u, w_ref[:, C_R:C_GA])).astype(BF16)
        gr_ref[p, :] = jax.nn.sigmoid(_dot(u, w_ref[:, C_GA:])).astype(BF16)
        v_ref[p, :] = _dot(u, w_ref[:, C_K:C_V]).astype(BF16)
        base = jnp.full((1, LANES), tile_pos + p.start, jnp.int32).astype(F32) * freq
        cos_b, sin_b = jnp.cos(base), jnp.sin(base)
        cos_r, sin_r = cosr_ref[...], sinr_ref[...]
        cos_t = cos_b * cos_r - sin_b * sin_r
        sin = sin_b * cos_r + cos_b * sin_r
        sin_lo, sin_hi = sin * sign_lo, sin * sign_hi
        q = _dot(u, w_ref[:, :C_Q])
        q_ref[p, :] = (_rope(q, cos_t, sin_lo, sin_hi) * (LOG2_E * HEAD_DIM ** -0.5)).astype(BF16)
        k = _dot(u, w_ref[:, C_Q:C_K])
        k_ref[p, :] = _rope(k, cos_t, sin_lo, sin_hi).astype(BF16)
    zlast_ref[...] = carry


def _inproj(h, gain, w_in, rope_consts, mu, pos_base, seq_rows):
    rope, cos_r, sin_r = rope_consts
    rows = h.shape[0]
    tm = min(ROW_TILE, rows)
    ntiles = rows // tm
    sub_rows = tm // SUB_TILES

    def row(i):
        return (i, 0)

    widths = (ATT_Q, ATT_KV, ATT_KV, RWKV_IN, D_MODEL, D_MODEL)
    dtypes = (BF16, BF16, BF16, F32, BF16, BF16)
    edge_spec = pl.BlockSpec((None, 1, RWKV_IN), lambda i: (i, 0, 0))
    edge_shape = jax.ShapeDtypeStruct((ntiles, 1, RWKV_IN), F32)
    out_specs = [pl.BlockSpec((tm, w), row) for w in widths]
    out_shape = [jax.ShapeDtypeStruct((rows, w), d) for w, d in zip(widths, dtypes)]
    q, k, v, zs, zfirst, zlast, ga, gr = pl.pallas_call(
        functools.partial(_inproj_kernel, pos_base=pos_base, tiles_per_seq=seq_rows // tm),
        grid=(ntiles,),
        in_specs=[
            pl.BlockSpec((tm, D_MODEL), row),
            _const_spec((1, D_MODEL)),
            _const_spec((D_MODEL, IN_COLS)),
            _const_spec((SUBLANES, LANES)),
            _const_spec((sub_rows, LANES)),
            _const_spec((sub_rows, LANES)),
            _const_spec((1, RWKV_IN)),
        ],
        out_specs=out_specs[:4] + [edge_spec, edge_spec] + out_specs[4:],
        out_shape=out_shape[:4] + [edge_shape, edge_shape] + out_shape[4:],
        compiler_params=pltpu.CompilerParams(
            dimension_semantics=("parallel",), vmem_limit_bytes=VMEM_LIMIT),
        name="inproj",
    )(h, gain, w_in, rope, cos_r[:sub_rows], sin_r[:sub_rows], mu)
    return q, k, v, (zs, zfirst, zlast, tm // BLOCK), ga, gr


def _rope_consts():
    half = ROT_DIM // 2
    inv_freq = 1.0 / (ROPE_THETA ** (jnp.arange(half, dtype=F32) * (2.0 / ROT_DIM)))
    lane = np.arange(LANES) % HEAD_DIM
    freq = jnp.where(lane < ROT_DIM, inv_freq[lane % half], 0.0)
    sign_lo = jnp.asarray(np.where(lane < half, -1.0, 0.0), F32)
    sign_hi = jnp.asarray(np.where((lane >= half) & (lane < ROT_DIM), 1.0, 0.0), F32)
    rope = jnp.concatenate([jnp.stack([freq, sign_lo, sign_hi]),
                            jnp.zeros((SUBLANES - 3, LANES), F32)], axis=0)
    offs = jnp.arange(ROW_TILE // SUB_TILES, dtype=F32)[:, None] * freq[None, :]
    return rope, jnp.cos(offs), jnp.sin(offs)


ATT_TILE_BLOCKS = 4


def _attn_kernel(sink_ref, q_ref, k_ref, v_ref, kp_ref, vp_ref, km_ref, vm_ref, o_ref):
    first = pl.program_id(1) == 0
    k_prev = jnp.where(first, km_ref[...], kp_ref[...])
    v_prev = jnp.where(first, vm_ref[...], vp_ref[...])
    k_all = jnp.concatenate([k_prev, k_ref[...]], axis=0)
    v_all = jnp.concatenate([v_prev, v_ref[...]], axis=0)
    row = lax.broadcasted_iota(jnp.int32, (BLOCK, 2 * BLOCK), 0)
    col = lax.broadcasted_iota(jnp.int32, (BLOCK, 2 * BLOCK), 1)
    mask = (col > row) & (col <= row + WINDOW)
    mask_first = (col > jnp.where(first, jnp.maximum(row, META_PAD - 1), row)) & (col <= row + WINDOW)
    sink_col = col == 0
    sink_row = lax.broadcasted_iota(jnp.int32, (2 * BLOCK, ATT_KV), 0) == 0

    pairs = [(j, g) for j in range(ATT_TILE_BLOCKS) for g in range(ATT_KV_HEADS)]
    scores = {}
    for j, g in pairs:
        qj = q_ref[j * BLOCK:(j + 1) * BLOCK, :]
        kg = k_all[j * BLOCK:(j + 2) * BLOCK, g * HEAD_DIM:(g + 1) * HEAD_DIM]
        q4 = jnp.concatenate([qj[:, hd * HEAD_DIM:(hd + 1) * HEAD_DIM]
                              for hd in range(g * ATT_GROUP, (g + 1) * ATT_GROUP)], axis=0)
        scores[j, g] = lax.dot_general(q4, kg, (((1,), (1,)), ((), ())),
                                       preferred_element_type=F32)
    logits, maxima = {}, {}
    for j, g in pairs:
        mj = mask_first if j == 0 else mask
        for i in range(ATT_GROUP):
            sh = jnp.where(mj, scores[j, g][i * BLOCK:(i + 1) * BLOCK], MASK_VALUE)
            sh = jnp.where(sink_col, sink_ref[g * ATT_GROUP + i] * LOG2_E, sh)
            logits[j, g, i] = sh
            maxima[j, g, i] = jnp.max(sh, axis=-1, keepdims=True)
    probs, denoms = {}, {}
    for key, sh in logits.items():
        p = jnp.exp2(sh - maxima[key])
        denoms[key] = jnp.sum(p, axis=-1, keepdims=True)
        probs[key] = p.astype(BF16)
    for j in range(ATT_TILE_BLOCKS):
        vj = jnp.where(sink_row, 0.0, v_all[j * BLOCK:(j + 2) * BLOCK].astype(F32)).astype(BF16)
        outs = []
        for g in range(ATT_KV_HEADS):
            p4 = jnp.concatenate([probs[j, g, i] for i in range(ATT_GROUP)], axis=0)
            o4 = _dot(p4, vj[:, g * HEAD_DIM:(g + 1) * HEAD_DIM])
            outs += [o4[i * BLOCK:(i + 1) * BLOCK] / denoms[j, g, i] for i in range(ATT_GROUP)]
        o_ref[j * BLOCK:(j + 1) * BLOCK, :] = jnp.concatenate(outs, axis=1).astype(BF16)


def _attention(sinks, q, k, v, k_meta, v_meta, batch, nblk):
    tile = ATT_TILE_BLOCKS * BLOCK
    ntile = nblk // ATT_TILE_BLOCKS

    def own(b, n):
        return (b * ntile + n, 0)

    def prev(b, n):
        return (b * nblk + jnp.maximum(n * ATT_TILE_BLOCKS - 1, 0), 0)

    return pl.pallas_call(
        _attn_kernel,
        grid=(batch, ntile),
        in_specs=[
            pl.BlockSpec(memory_space=pltpu.SMEM),
            pl.BlockSpec((tile, ATT_Q), own),
            pl.BlockSpec((tile, ATT_KV), own),
            pl.BlockSpec((tile, ATT_KV), own),
            pl.BlockSpec((BLOCK, ATT_KV), prev),
            pl.BlockSpec((BLOCK, ATT_KV), prev),
            _const_spec((BLOCK, ATT_KV)),
            _const_spec((BLOCK, ATT_KV)),
        ],
        out_specs=pl.BlockSpec((tile, ATT_Q), own),
        out_shape=jax.ShapeDtypeStruct((batch * nblk * BLOCK, ATT_Q), BF16),
        compiler_params=pltpu.CompilerParams(
            dimension_semantics=("parallel", "parallel"), vmem_limit_bytes=VMEM_LIMIT),
        name="attention",
    )(sinks, q, k, v, k, v, k_meta, v_meta)


O_K = RWKV_WIDTH
O_V = 2 * RWKV_WIDTH
O_W = 3 * RWKV_WIDTH
O_A = O_W + DECAY_LORA
O_G = O_A + AAA_LORA

GROUP_HEADS = 4
GROUP_W = GROUP_HEADS * RWKV_HEAD
N_GROUPS = RWKV_HEADS // GROUP_HEADS
CUMSUM_PARTS = 2
INV_BASE = 8
STAGE_SKEW = 1


def _mm(a, b):
    return jnp.dot(a.astype(BF16), b.astype(BF16), preferred_element_type=F32)


def _mm_nt(a, b):
    return lax.dot_general(a.astype(BF16), b.astype(BF16), (((1,), (1,)), ((), ())),
                           preferred_element_type=F32)


def _mm_tn(a, b):
    return lax.dot_general(a, b, (((0,), (0,)), ((), ())), preferred_element_type=F32)


def _sigmoid(x):
    return 0.5 * jnp.tanh(0.5 * x) + 0.5


DIAG_TILES = [(slice(h * RWKV_HEAD, (h + 1) * RWKV_HEAD),
               slice((h // 2) * LANES, (h // 2 + 1) * LANES)) for h in range(GROUP_HEADS)]


def _diag_tiles(x, bd_mask):
    return [x[:, lanes] * bd_mask[rows, lanes] for rows, lanes in DIAG_TILES]


def _from_diag_tiles(tiles):
    zero = jnp.zeros_like(tiles[0])
    return jnp.concatenate(
        [jnp.concatenate([t, zero] if h // 2 == 0 else [zero, t], axis=1)
         for h, t in enumerate(tiles)], axis=0)


def _block_diag(x, bd_mask):
    return _from_diag_tiles(_diag_tiles(x.astype(BF16), bd_mask))


def _head_sum(x, bd_mask):
    return jnp.concatenate(
        [_mm(x[:, i * GROUP_W:(i + 1) * GROUP_W], bd_mask) for i in range(N_GROUPS)], axis=1)


def _chunk_cumsum(tri, x):
    pieces, rem = [], x
    for i in range(CUMSUM_PARTS):
        hi = rem.astype(BF16)
        pieces.append(hi)
        if i + 1 < CUMSUM_PARTS:
            rem = rem - hi.astype(F32)
    out = jnp.dot(tri, jnp.concatenate(pieces, axis=1), preferred_element_type=F32)
    w = x.shape[1]
    return sum(out[:, i * w:(i + 1) * w] for i in range(CUMSUM_PARTS))


def _rwkv_kernel(zs_ref, zfirst_ref, zlast_ref, zprev0_ref, s0_ref, tri_ref, bd_ref, mu_ref, w0_ref,
                 w2_ref, a0_ref, a2_ref, g2_ref, kk_ref, ka_ref, rk_ref, lnw_ref, lnb_ref,
                 y_ref, sout_ref, state_scr, ybuf, *, blocks_per_tile):
    nb = zs_ref.shape[0]
    n = pl.program_id(0)
    tiles_per_seq = pl.num_programs(0) // blocks_per_tile

    @pl.when(n == 0)
    def _():
        for b in range(nb):
            state_scr[b] = s0_ref[...]

    bd_mask = bd_ref[...]
    bd_f32 = bd_mask.astype(F32)
    tri = tri_ref[...]
    tile_start = n % blocks_per_tile == 0
    first_row = lax.broadcasted_iota(jnp.int32, (SUBLANES, RWKV_IN), 0) == 0
    ti = lax.broadcasted_iota(jnp.int32, (CHUNK, GROUP_W), 0)
    si = lax.broadcasted_iota(jnp.int32, (CHUNK, GROUP_W), 1) % CHUNK
    strict = ti > si
    incl = ti >= si
    eye = (ti == si).astype(F32)

    def same_block(size):
        return (ti // size) == (si // size)

    base_mask = strict & same_block(INV_BASE)
    merge_masks = []
    size = INV_BASE
    while size < CHUNK:
        merge_masks.append(strict & same_block(2 * size) & jnp.logical_not(same_block(size)))
        size *= 2
    n_chunks = BLOCK // CHUNK
    groups = range(N_GROUPS)
    chains = [(c, gi) for c in range(n_chunks) for gi in groups]

    def batch_stages(b):
        zs = zs_ref[b]
        tile = b * tiles_per_seq + n // blocks_per_tile
        z0 = zfirst_ref[tile]
        z0_prev = jnp.where(n == 0, zprev0_ref[...], zlast_ref[jnp.maximum(tile - 1, 0)])
        row0 = jnp.where(tile_start, z0 + (z0_prev - z0) * mu_ref[...], zs[0:1, :])
        zs = jnp.concatenate([jnp.where(first_row, row0, zs[:SUBLANES, :]), zs[SUBLANES:, :]], axis=0)
        r = zs[:, :O_K]
        k = zs[:, O_K:O_V]
        v = zs[:, O_V:O_W]
        ld = -DECAY_SCALE * _sigmoid(w0_ref[...] + _mm(jnp.tanh(zs[:, O_W:O_A]), w2_ref[...]))
        a = _sigmoid(a0_ref[...] + _mm(zs[:, O_A:O_G], a2_ref[...]))
        g = _mm(_sigmoid(zs[:, O_G:]), g2_ref[...])
        kk = k * kk_ref[...]
        kk_sq = _head_sum(kk * kk, bd_mask)
        yield
        kk = kk * lax.rsqrt(jnp.maximum(kk_sq, 1e-24))
        k = k * (1.0 + (a - 1.0) * ka_ref[...])
        cum = _chunk_cumsum(tri, ld)
        yield
        e_pos = jnp.exp(cum)
        e_neg = jnp.exp(-cum)
        r_t = r * e_pos
        k_t = k * e_neg
        a_t = -kk * jnp.exp(cum - ld)
        b_t = kk * a * e_neg

        def part(x, ch):
            c, gi = ch
            return x[c * CHUNK:(c + 1) * CHUNK, gi * GROUP_W:(gi + 1) * GROUP_W]

        ar, m_ab, m_ak, m_rb, m_rk = {}, {}, {}, {}, {}
        for ch in chains:
            ar[ch] = jnp.concatenate([part(a_t, ch), part(r_t, ch)], axis=0).astype(BF16)
            bd_bk = jnp.concatenate([_block_diag(part(b_t, ch), bd_mask),
                                     _block_diag(part(k_t, ch), bd_mask)], axis=0)
            sc = _mm_nt(ar[ch], bd_bk)
            m_ab[ch] = sc[:CHUNK, :GROUP_W]
            m_ak[ch] = jnp.where(strict, sc[:CHUNK, GROUP_W:], 0.0)
            m_rb[ch] = jnp.where(incl, sc[CHUNK:, :GROUP_W], 0.0)
            m_rk[ch] = jnp.where(incl, sc[CHUNK:, GROUP_W:], 0.0)
        yield
        low = {ch: jnp.where(base_mask, m_ab[ch], 0.0) for ch in chains}
        pw = {ch: _mm(low[ch], _block_diag(low[ch], bd_mask)) for ch in chains}
        yield
        inv = {ch: eye + low[ch] for ch in chains}
        for _ in range(INV_BASE.bit_length() - 3):
            for ch in chains:
                both = _mm(jnp.concatenate([pw[ch], inv[ch]], axis=0), _block_diag(pw[ch], bd_mask))
                pw[ch] = both[:CHUNK]
                inv[ch] = inv[ch] + both[CHUNK:]
            yield
        for ch in chains:
            inv[ch] = inv[ch] + _mm(inv[ch], _block_diag(pw[ch], bd_mask))
        yield
        joins = {}
        for ch in chains:
            lo_all = jnp.concatenate([jnp.where(mask, m_ab[ch], 0.0) for mask in merge_masks], axis=0)
            prod = _mm(lo_all, _block_diag(inv[ch], bd_mask))
            joins[ch] = [prod[j * CHUNK:(j + 1) * CHUNK] for j in range(len(merge_masks))]
        yield
        for level in range(len(merge_masks)):
            for ch in chains:
                later = joins[ch][level + 1:]
                prod = _mm(jnp.concatenate([inv[ch]] + later, axis=0),
                           _block_diag(joins[ch][level], bd_mask))
                inv[ch] = inv[ch] + prod[:CHUNK]
                for j, z in enumerate(later):
                    joins[ch][level + 1 + j] = z + prod[(j + 1) * CHUNK:(j + 2) * CHUNK]
            yield
        mv = {ch: _mm(jnp.concatenate([m_ak[ch], m_rk[ch]], axis=0),
                      _block_diag(part(v, ch), bd_mask)) for ch in chains}
        yield
        state = {gi: [state_scr[(b, gi) + tile] for tile in DIAG_TILES] for gi in groups}
        for c in range(n_chunks):
            cur = [(c, gi) for gi in groups]
            xr0 = {ch: _mm_nt(ar[ch], _from_diag_tiles([t.astype(BF16) for t in state[ch[1]]]))
                   for ch in cur}
            yield
            u = {ch: _mm(inv[ch], _block_diag(xr0[ch][:CHUNK] + mv[ch][:CHUNK], bd_mask))
                 for ch in cur}
            yield
            for ch in cur:
                gi = ch[1]
                y = xr0[ch][CHUNK:] + mv[ch][CHUNK:] + _mm(m_rb[ch], _block_diag(u[ch], bd_mask))
                upd = _mm_tn(jnp.concatenate([u[ch], part(v, ch)], axis=0),
                             jnp.concatenate([part(b_t, ch), part(k_t, ch)], axis=0))
                last = (c + 1) * CHUNK - 1
                w_end = e_pos[last:last + 1, gi * GROUP_W:(gi + 1) * GROUP_W]
                state[gi] = [(s + upd[tile] * bd_f32[tile]) * w_end[:, tile[1]]
                             for s, tile in zip(state[gi], DIAG_TILES)]
                ybuf[b, c * CHUNK:(c + 1) * CHUNK, gi * GROUP_W:(gi + 1) * GROUP_W] = y
            yield
        for gi in groups:
            for s, tile in zip(state[gi], DIAG_TILES):
                state_scr[(b, gi) + tile] = s
            if b == 0:
                sout_ref[gi] = _from_diag_tiles(state[gi])
        y = ybuf[b]
        mean = _head_sum(y, bd_mask) * (1.0 / RWKV_HEAD)
        bonus = _head_sum(r * k * rk_ref[...], bd_mask) * v
        yield
        yc = y - mean
        var = _head_sum(yc * yc, bd_mask) * (1.0 / RWKV_HEAD)
        yield
        yn = yc * lax.rsqrt(var + GN_EPS) * lnw_ref[...] + lnb_ref[...]
        y_ref[b] = ((yn + bonus) * g).astype(BF16)

    stages = [batch_stages(b) for b in range(nb)]
    live = set(range(nb))
    tick = 0
    while live:
        for b in sorted(live):
            if tick >= b * STAGE_SKEW and next(stages[b], "done") == "done":
                live.discard(b)
        tick += 1


def _rwkv(shifted, batch, zprev0, s0, tri, bd_mask, params):
    zs, zfirst, zlast, blocks_per_tile = shifted
    zs = zs.reshape(batch, -1, RWKV_IN)
    rows = zs.shape[1]
    state_shape = (N_GROUPS, GROUP_W, GROUP_W)
    return pl.pallas_call(
        functools.partial(_rwkv_kernel, blocks_per_tile=blocks_per_tile),
        grid=(rows // BLOCK,),
        in_specs=[
            pl.BlockSpec((batch, BLOCK, RWKV_IN), lambda n: (0, n, 0)),
            _const_spec(zfirst.shape),
            _const_spec(zlast.shape),
            _const_spec((1, RWKV_IN)),
            _const_spec(state_shape),
            _const_spec((BLOCK, BLOCK)),
            _const_spec((GROUP_W, GROUP_W)),
        ] + [_const_spec(p.shape) for p in params],
        out_specs=[
            pl.BlockSpec((batch, BLOCK, RWKV_WIDTH), lambda n: (0, n, 0)),
            pl.BlockSpec(state_shape, lambda n: (0, 0, 0)),
        ],
        out_shape=[
            jax.ShapeDtypeStruct((batch, rows, RWKV_WIDTH), BF16),
            jax.ShapeDtypeStruct(state_shape, F32),
        ],
        scratch_shapes=[
            pltpu.VMEM((batch,) + state_shape, F32),
            pltpu.VMEM((batch, BLOCK, RWKV_WIDTH), F32),
        ],
        compiler_params=pltpu.CompilerParams(
            dimension_semantics=("arbitrary",), vmem_limit_bytes=VMEM_LIMIT),
        name="rwkv",
    )(zs, zfirst, zlast, zprev0, s0, tri, bd_mask, *params)


def kernel(x, meta_tokens, ffn1_norm_pre, ffn1_w_gate_up, ffn1_w_down, ffn1_norm_post, mix_norm_pre, w_in, att_sinks, rwkv_mu, rwkv_w0, rwkv_w2, rwkv_a0, rwkv_a2, rwkv_g2, rwkv_k_k, rwkv_k_a, rwkv_r_k, rwkv_ln_w, rwkv_ln_b, w_att_branch, w_rwkv_branch, w_mix_out, mix_norm_post, ffn2_norm_pre, ffn2_w_gate_up, ffn2_w_down, ffn2_norm_post):
    batch, seq, _ = x.shape
    nblk = seq // BLOCK
    depth = w_in.shape[0]
    assert depth == 1 and seq % BLOCK == 0

    rope = _rope_consts()
    ri = jnp.arange(BLOCK)
    tri = ((ri[:, None] >= ri[None, :]) &
           (ri[:, None] // CHUNK == ri[None, :] // CHUNK)).astype(BF16)
    li = jnp.arange(GROUP_W) // RWKV_HEAD
    bd_mask = (li[:, None] == li[None, :]).astype(BF16)

    hx = x.reshape(batch * seq, D_MODEL)
    hm = jnp.concatenate([jnp.zeros((META_PAD, D_MODEL), x.dtype),
                          meta_tokens.astype(x.dtype)], axis=0)

    def row(p):
        return p.reshape(1, -1)

    l = 0
    wgu1, wd1 = ffn1_w_gate_up[l].astype(BF16), ffn1_w_down[l].astype(BF16)
    wgu2, wd2 = ffn2_w_gate_up[l].astype(BF16), ffn2_w_down[l].astype(BF16)
    w_in_l = w_in[l].astype(BF16)
    wa, wr, wo = (w_att_branch[l].astype(BF16), w_rwkv_branch[l].astype(BF16),
                  w_mix_out[l].astype(BF16))
    rw_params = (row(rwkv_mu[l]), row(rwkv_w0[l]), rwkv_w2[l], row(rwkv_a0[l]), rwkv_a2[l],
                 rwkv_g2[l], row(rwkv_k_k[l]), row(rwkv_k_a[l]), row(rwkv_r_k[l]),
                 row(rwkv_ln_w[l]), row(rwkv_ln_b[l]))

    outs = []
    for h, pos_base, seq_rows in ((hm, -META_PAD, BLOCK), (hx, N_META, seq)):
        h1 = _ffn(h, row(ffn1_norm_pre[l]), wgu1, wd1, row(ffn1_norm_post[l]))
        outs.append((h1,) + _inproj(h1, row(mix_norm_pre[l]), w_in_l, rope, row(rwkv_mu[l]),
                                    pos_base, seq_rows))
    (_, _, km, vm, shifted_m, _, _), (hx1, q, k, v, shifted_x, ga, gr) = outs

    o_att = _attention(att_sinks[l], q, k, v, km, vm, batch, nblk)

    zero_state = jnp.zeros((N_GROUPS, GROUP_W, GROUP_W), F32)
    _, s_meta = _rwkv(shifted_m, 1, jnp.zeros((1, RWKV_IN), F32), zero_state, tri, bd_mask,
                      rw_params)
    y_rwkv, _ = _rwkv(shifted_x, batch, shifted_m[2][0], s_meta, tri, bd_mask, rw_params)
    y_rwkv = y_rwkv.reshape(batch * seq, RWKV_WIDTH)

    hx = _ffn(hx1, row(ffn2_norm_pre[l]), wgu2, wd2, row(ffn2_norm_post[l]),
              merge=(o_att, y_rwkv, ga, gr, wa, wr, wo, row(mix_norm_post[l])))
    return hx.reshape(batch, seq, D_MODEL)
```

```python
import functools

import jax
import jax.numpy as jnp
import numpy as np
from jax import lax
from jax.experimental import pallas as pl
from jax.experimental.pallas import tpu as pltpu

D_MODEL = 1024
N_META = 16
ATT_HEADS = 8
ATT_KV_HEADS = 2
ATT_GROUP = ATT_HEADS // ATT_KV_HEADS
HEAD_DIM = 64
WINDOW = 128
BLOCK = 128
ROPE_THETA = 500000.0
ROT_DIM = HEAD_DIM // 4
ATT_Q = ATT_HEADS * HEAD_DIM
ATT_KV = ATT_KV_HEADS * HEAD_DIM
MASK_VALUE = -1e30
RWKV_HEADS = 8
RWKV_HEAD = 64
RWKV_WIDTH = RWKV_HEADS * RWKV_HEAD
DECAY_LORA = 64
AAA_LORA = 64
GATE_LORA = 128
GN_EPS = 64e-5
RWKV_IN = 3 * RWKV_WIDTH + DECAY_LORA + AAA_LORA + GATE_LORA
D_FF = 2816
NORM_EPS = 1e-6

META_PAD = BLOCK - N_META
CHUNK = 64
DECAY_SCALE = 0.6065306597126334
LOG2_E = 1.4426950408889634

LANES, SUBLANES = 128, 8
V7X_VMEM_BYTES = 64 * 1024 * 1024
VMEM_LIMIT = V7X_VMEM_BYTES - 8 * 1024 * 1024
ROW_TILE = 512
SUB_TILES = 2

F32 = jnp.float32
BF16 = jnp.bfloat16


def _rms(x, gain):
    ms = jnp.mean(x * x, axis=-1, keepdims=True)
    return x * lax.rsqrt(ms + NORM_EPS) * gain


def _dot(a, b):
    return jnp.dot(a, b, preferred_element_type=F32)


def _const_spec(shape):
    nd = len(shape)
    return pl.BlockSpec(shape, lambda *_: (0,) * nd, pipeline_mode=pl.Buffered(1))


def _sub_tiles(rows):
    sub = rows // SUB_TILES
    return [slice(i * sub, (i + 1) * sub) for i in range(SUB_TILES)]


def _ffn_kernel(*refs, with_merge):
    if with_merge:
        (h_ref, oa_ref, yr_ref, ga_ref, gr_ref, wa_ref, wr_ref, wo_ref, gmix_ref,
         gpre_ref, wgu_ref, wd_ref, gpost_ref, o_ref) = refs
    else:
        h_ref, gpre_ref, wgu_ref, wd_ref, gpost_ref, o_ref = refs
    parts = _sub_tiles(h_ref.shape[0])
    if with_merge:
        merged = []
        for p in parts:
            y_att = _dot(oa_ref[p, :], wa_ref[...])
            y_rwkv = _dot(yr_ref[p, :], wr_ref[...])
            merged.append((ga_ref[p, :].astype(F32) * y_att
                           + gr_ref[p, :].astype(F32) * y_rwkv).astype(BF16))
        hs = [h_ref[p, :] + _rms(_dot(m, wo_ref[...]), gmix_ref[...])
              for p, m in zip(parts, merged)]
    else:
        hs = [h_ref[p, :] for p in parts]
    acts = []
    for h in hs:
        hn = _rms(h, gpre_ref[...]).astype(BF16)
        gate = _dot(hn, wgu_ref[:, :D_FF])
        up = _dot(hn, wgu_ref[:, D_FF:])
        acts.append((gate * jax.nn.sigmoid(gate) * up).astype(BF16))
    for p, h, act in zip(parts, hs, acts):
        f = _dot(act, wd_ref[...])
        o_ref[p, :] = h + 0.5 * _rms(f, gpost_ref[...])


def _ffn(h, gpre, wgu, wd, gpost, merge=None):
    rows = h.shape[0]
    tm = min(ROW_TILE, rows)

    def row(i):
        return (i, 0)

    operands = [h]
    in_specs = [pl.BlockSpec((tm, D_MODEL), row)]
    if merge is not None:
        o_att, y_rwkv, ga, gr, wa, wr, wo, gmix = merge
        operands += [o_att, y_rwkv, ga, gr, wa, wr, wo, gmix]
        in_specs += [
            pl.BlockSpec((tm, ATT_Q), row),
            pl.BlockSpec((tm, RWKV_WIDTH), row),
            pl.BlockSpec((tm, D_MODEL), row),
            pl.BlockSpec((tm, D_MODEL), row),
            _const_spec((ATT_Q, D_MODEL)),
            _const_spec((RWKV_WIDTH, D_MODEL)),
            _const_spec((D_MODEL, D_MODEL)),
            _const_spec((1, D_MODEL)),
        ]
    operands += [gpre, wgu, wd, gpost]
    in_specs += [
        _const_spec((1, D_MODEL)),
        _const_spec((D_MODEL, 2 * D_FF)),
        _const_spec((D_FF, D_MODEL)),
        _const_spec((1, D_MODEL)),
    ]
    return pl.pallas_call(
        functools.partial(_ffn_kernel, with_merge=merge is not None),
        grid=(rows // tm,),
        in_specs=in_specs,
        out_specs=pl.BlockSpec((tm, D_MODEL), row),
        out_shape=jax.ShapeDtypeStruct((rows, D_MODEL), F32),
        compiler_params=pltpu.CompilerParams(
            dimension_semantics=("parallel",), vmem_limit_bytes=VMEM_LIMIT),
        name="merge_ffn" if merge is not None else "ffn",
    )(*operands)


C_Q = ATT_Q
C_K = C_Q + ATT_KV
C_V = C_K + ATT_KV
C_R = C_V + RWKV_IN
C_GA = C_R + D_MODEL
IN_COLS = C_GA + D_MODEL


def _rope(x, cos_t, sin_lo, sin_hi):
    n = x.shape[1] // LANES
    if n > 1:
        cos_t = jnp.concatenate([cos_t] * n, axis=1)
        sin_lo = jnp.concatenate([sin_lo] * n, axis=1)
        sin_hi = jnp.concatenate([sin_hi] * n, axis=1)
    width = x.shape[1]
    half = ROT_DIM // 2
    from_hi = pltpu.roll(x, width - half, 1)
    from_lo = pltpu.roll(x, half, 1)
    return x * cos_t + from_hi * sin_lo + from_lo * sin_hi


def _inproj_kernel(h_ref, g_ref, w_ref, rope_ref, cosr_ref, sinr_ref, mu_ref,
                   q_ref, k_ref, v_ref, zs_ref, zfirst_ref, zlast_ref, ga_ref, gr_ref,
                   *, pos_base, tiles_per_seq):
    tm = h_ref.shape[0]
    tile_pos = pos_base + (pl.program_id(0) % tiles_per_seq) * tm
    freq, sign_lo, sign_hi = rope_ref[0:1, :], rope_ref[1:2, :], rope_ref[2:3, :]
    carry = None
    for p in _sub_tiles(tm):
        rows = p.stop - p.start
        u = _rms(h_ref[p, :], g_ref[...]).astype(BF16)
        z = _dot(u, w_ref[:, C_V:C_R])
        z_prev = pltpu.roll(z, 1, 0)
        if carry is None:
            zfirst_ref[...] = z[0:1, :]
        else:
            row_id = lax.broadcasted_iota(jnp.int32, z.shape, 0)
            z_prev = jnp.where(row_id == 0, carry, z_prev)
        zs_ref[p, :] = z + (z_prev - z) * mu_ref[...]
        carry = z[rows - 1:rows, :]
        ga_ref[p, :] = jax.nn.sigmoid(_dot(u, w_ref[:, C_R:C_GA])).astype(BF16)
        gr_ref[p, :] = jax.nn.sigmoid(_dot(u, w_ref[:, C_GA:])).astype(BF16)
        base = jnp.full((1, LANES), tile_pos + p.start, jnp.int32).astype(F32) * freq
        cos_b, sin_b = jnp.cos(base), jnp.sin(base)
        cos_r, sin_r = cosr_ref[...], sinr_ref[...]
        cos_t = cos_b * cos_r - sin_b * sin_r
        sin = sin_b * cos_r + cos_b * sin_r
        sin_lo, sin_hi = sin * sign_lo, sin * sign_hi
        q = _dot(u, w_ref[:, :C_Q])
        q_ref[p, :] = (_rope(q, cos_t, sin_lo, sin_hi) * (LOG2_E * HEAD_DIM ** -0.5)).astype(BF16)
        kv = _dot(u, w_ref[:, C_Q:C_V])
        k_ref[p, :] = _rope(kv[:, :ATT_KV], cos_t, sin_lo, sin_hi).astype(BF16)
        v_ref[p, :] = kv[:, ATT_KV:].astype(BF16)
    zlast_ref[...] = carry


def _inproj(h, gain, w_in, rope_consts, mu, pos_base, seq_rows):
    rope, cos_r, sin_r = rope_consts
    rows = h.shape[0]
    tm = min(ROW_TILE, rows)
    ntiles = rows // tm
    sub_rows = tm // SUB_TILES

    def row(i):
        return (i, 0)

    widths = (ATT_Q, ATT_KV, ATT_KV, RWKV_IN, D_MODEL, D_MODEL)
    dtypes = (BF16, BF16, BF16, F32, BF16, BF16)
    edge_spec = pl.BlockSpec((None, 1, RWKV_IN), lambda i: (i, 0, 0))
    edge_shape = jax.ShapeDtypeStruct((ntiles, 1, RWKV_IN), F32)
    out_specs = [pl.BlockSpec((tm, w), row) for w in widths]
    out_shape = [jax.ShapeDtypeStruct((rows, w), d) for w, d in zip(widths, dtypes)]
    q, k, v, zs, zfirst, zlast, ga, gr = pl.pallas_call(
        functools.partial(_inproj_kernel, pos_base=pos_base, tiles_per_seq=seq_rows // tm),
        grid=(ntiles,),
        in_specs=[
            pl.BlockSpec((tm, D_MODEL), row),
            _const_spec((1, D_MODEL)),
            _const_spec((D_MODEL, IN_COLS)),
            _const_spec((SUBLANES, LANES)),
            _const_spec((sub_rows, LANES)),
            _const_spec((sub_rows, LANES)),
            _const_spec((1, RWKV_IN)),
        ],
        out_specs=out_specs[:4] + [edge_spec, edge_spec] + out_specs[4:],
        out_shape=out_shape[:4] + [edge_shape, edge_shape] + out_shape[4:],
        compiler_params=pltpu.CompilerParams(
            dimension_semantics=("parallel",), vmem_limit_bytes=VMEM_LIMIT),
        name="inproj",
    )(h, gain, w_in, rope, cos_r[:sub_rows], sin_r[:sub_rows], mu)
    return q, k, v, (zs, zfirst, zlast, tm // BLOCK), ga, gr


def _rope_consts():
    half = ROT_DIM // 2
    inv_freq = 1.0 / (ROPE_THETA ** (jnp.arange(half, dtype=F32) * (2.0 / ROT_DIM)))
    lane = np.arange(LANES) % HEAD_DIM
    freq = jnp.where(lane < ROT_DIM, inv_freq[lane % half], 0.0)
    sign_lo = jnp.asarray(np.where(lane < half, -1.0, 0.0), F32)
    sign_hi = jnp.asarray(np.where((lane >= half) & (lane < ROT_DIM), 1.0, 0.0), F32)
    rope = jnp.concatenate([jnp.stack([freq, sign_lo, sign_hi]),
                            jnp.zeros((SUBLANES - 3, LANES), F32)], axis=0)
    offs = jnp.arange(ROW_TILE // SUB_TILES, dtype=F32)[:, None] * freq[None, :]
    return rope, jnp.cos(offs), jnp.sin(offs)


ATT_TILE_BLOCKS = 4


def _attn_kernel(sink_ref, q_ref, k_ref, v_ref, kp_ref, vp_ref, km_ref, vm_ref, o_ref):
    first = pl.program_id(1) == 0
    k_prev = jnp.where(first, km_ref[...], kp_ref[...])
    v_prev = jnp.where(first, vm_ref[...], vp_ref[...])
    k_all = jnp.concatenate([k_prev, k_ref[...]], axis=0)
    v_all = jnp.concatenate([v_prev, v_ref[...]], axis=0)
    row = lax.broadcasted_iota(jnp.int32, (BLOCK, 2 * BLOCK), 0)
    col = lax.broadcasted_iota(jnp.int32, (BLOCK, 2 * BLOCK), 1)
    mask = (col > row) & (col <= row + WINDOW)
    mask_first = (col > jnp.where(first, jnp.maximum(row, META_PAD - 1), row)) & (col <= row + WINDOW)
    sink_col = col == 0
    sink_row = lax.broadcasted_iota(jnp.int32, (2 * BLOCK, ATT_KV), 0) == 0

    pairs = [(j, g) for j in range(ATT_TILE_BLOCKS) for g in range(ATT_KV_HEADS)]
    scores = {}
    for j, g in pairs:
        qj = q_ref[j * BLOCK:(j + 1) * BLOCK, :]
        kg = k_all[j * BLOCK:(j + 2) * BLOCK, g * HEAD_DIM:(g + 1) * HEAD_DIM]
        q4 = jnp.concatenate([qj[:, hd * HEAD_DIM:(hd + 1) * HEAD_DIM]
                              for hd in range(g * ATT_GROUP, (g + 1) * ATT_GROUP)], axis=0)
        scores[j, g] = lax.dot_general(q4, kg, (((1,), (1,)), ((), ())),
                                       preferred_element_type=F32)
    logits, maxima = {}, {}
    for j, g in pairs:
        mj = mask_first if j == 0 else mask
        for i in range(ATT_GROUP):
            sh = jnp.where(mj, scores[j, g][i * BLOCK:(i + 1) * BLOCK], MASK_VALUE)
            sh = jnp.where(sink_col, sink_ref[g * ATT_GROUP + i] * LOG2_E, sh)
            logits[j, g, i] = sh
            maxima[j, g, i] = jnp.max(sh, axis=-1, keepdims=True)
    probs, denoms = {}, {}
    for key, sh in logits.items():
        p = jnp.exp2(sh - maxima[key])
        denoms[key] = jnp.sum(p, axis=-1, keepdims=True)
        probs[key] = p.astype(BF16)
    for j in range(ATT_TILE_BLOCKS):
        vj = jnp.where(sink_row, 0.0, v_all[j * BLOCK:(j + 2) * BLOCK].astype(F32)).astype(BF16)
        outs = []
        for g in range(ATT_KV_HEADS):
            p4 = jnp.concatenate([probs[j, g, i] for i in range(ATT_GROUP)], axis=0)
            o4 = _dot(p4, vj[:, g * HEAD_DIM:(g + 1) * HEAD_DIM])
            outs += [o4[i * BLOCK:(i + 1) * BLOCK] / denoms[j, g, i] for i in range(ATT_GROUP)]
        o_ref[j * BLOCK:(j + 1) * BLOCK, :] = jnp.concatenate(outs, axis=1).astype(BF16)


def _attention(sinks, q, k, v, k_meta, v_meta, batch, nblk):
    tile = ATT_TILE_BLOCKS * BLOCK
    ntile = nblk // ATT_TILE_BLOCKS

    def own(b, n):
        return (b * ntile + n, 0)

    def prev(b, n):
        return (b * nblk + jnp.maximum(n * ATT_TILE_BLOCKS - 1, 0), 0)

    return pl.pallas_call(
        _attn_kernel,
        grid=(batch, ntile),
        in_specs=[
            pl.BlockSpec(memory_space=pltpu.SMEM),
            pl.BlockSpec((tile, ATT_Q), own),
            pl.BlockSpec((tile, ATT_KV), own),
            pl.BlockSpec((tile, ATT_KV), own),
            pl.BlockSpec((BLOCK, ATT_KV), prev),
            pl.BlockSpec((BLOCK, ATT_KV), prev),
            _const_spec((BLOCK, ATT_KV)),
            _const_spec((BLOCK, ATT_KV)),
        ],
        out_specs=pl.BlockSpec((tile, ATT_Q), own),
        out_shape=jax.ShapeDtypeStruct((batch * nblk * BLOCK, ATT_Q), BF16),
        compiler_params=pltpu.CompilerParams(
            dimension_semantics=("parallel", "parallel"), vmem_limit_bytes=VMEM_LIMIT),
        name="attention",
    )(sinks, q, k, v, k, v, k_meta, v_meta)


O_K = RWKV_WIDTH
O_V = 2 * RWKV_WIDTH
O_W = 3 * RWKV_WIDTH
O_A = O_W + DECAY_LORA
O_G = O_A + AAA_LORA

GROUP_HEADS = 4
GROUP_W = GROUP_HEADS * RWKV_HEAD
N_GROUPS = RWKV_HEADS // GROUP_HEADS
CUMSUM_PARTS = 2
INV_BASE = 8
STAGE_SKEW = 1


def _mm(a, b):
    return jnp.dot(a.astype(BF16), b.astype(BF16), preferred_element_type=F32)


def _mm_nt(a, b):
    return lax.dot_general(a.astype(BF16), b.astype(BF16), (((1,), (1,)), ((), ())),
                           preferred_element_type=F32)


def _mm_tn(a, b):
    return lax.dot_general(a, b, (((0,), (0,)), ((), ())), preferred_element_type=F32)


def _sigmoid(x):
    return 0.5 * jnp.tanh(0.5 * x) + 0.5


DIAG_TILES = [(slice(h * RWKV_HEAD, (h + 1) * RWKV_HEAD),
               slice((h // 2) * LANES, (h // 2 + 1) * LANES)) for h in range(GROUP_HEADS)]


def _diag_tiles(x, bd_mask):
    return [x[:, lanes] * bd_mask[rows, lanes] for rows, lanes in DIAG_TILES]


def _from_diag_tiles(tiles):
    zero = jnp.zeros_like(tiles[0])
    return jnp.concatenate(
        [jnp.concatenate([t, zero] if h // 2 == 0 else [zero, t], axis=1)
         for h, t in enumerate(tiles)], axis=0)


def _block_diag(x, bd_mask):
    return _from_diag_tiles(_diag_tiles(x.astype(BF16), bd_mask))


def _head_sum(x, bd_mask):
    return jnp.concatenate(
        [_mm(x[:, i * GROUP_W:(i + 1) * GROUP_W], bd_mask) for i in range(N_GROUPS)], axis=1)


def _chunk_cumsum(tri, x):
    pieces, rem = [], x
    for i in range(CUMSUM_PARTS):
        hi = rem.astype(BF16)
        pieces.append(hi)
        if i + 1 < CUMSUM_PARTS:
            rem = rem - hi.astype(F32)
    out = jnp.dot(tri, jnp.concatenate(pieces, axis=1), preferred_element_type=F32)
    w = x.shape[1]
    return sum(out[:, i * w:(i + 1) * w] for i in range(CUMSUM_PARTS))


def _rwkv_kernel(zs_ref, zfirst_ref, zlast_ref, zprev0_ref, s0_ref, tri_ref, bd_ref, mu_ref, w0_ref,
                 w2_ref, a0_ref, a2_ref, g2_ref, kk_ref, ka_ref, rk_ref, lnw_ref, lnb_ref,
                 y_ref, sout_ref, state_scr, ybuf, *, blocks_per_tile):
    nb = zs_ref.shape[0]
    n = pl.program_id(0)
    tiles_per_seq = pl.num_programs(0) // blocks_per_tile

    @pl.when(n == 0)
    def _():
        for b in range(nb):
            state_scr[b] = s0_ref[...]

    bd_mask = bd_ref[...]
    bd_f32 = bd_mask.astype(F32)
    tri = tri_ref[...]
    tile_start = n % blocks_per_tile == 0
    first_row = lax.broadcasted_iota(jnp.int32, (SUBLANES, RWKV_IN), 0) == 0
    ti = lax.broadcasted_iota(jnp.int32, (CHUNK, GROUP_W), 0)
    si = lax.broadcasted_iota(jnp.int32, (CHUNK, GROUP_W), 1) % CHUNK
    strict = ti > si
    incl = ti >= si
    eye = (ti == si).astype(F32)

    def same_block(size):
        return (ti // size) == (si // size)

    base_mask = strict & same_block(INV_BASE)
    merge_masks = []
    size = INV_BASE
    while size < CHUNK:
        merge_masks.append(strict & same_block(2 * size) & jnp.logical_not(same_block(size)))
        size *= 2
    n_chunks = BLOCK // CHUNK
    groups = range(N_GROUPS)
    chains = [(c, gi) for c in range(n_chunks) for gi in groups]

    def batch_stages(b):
        zs = zs_ref[b]
        tile = b * tiles_per_seq + n // blocks_per_tile
        z0 = zfirst_ref[tile]
        z0_prev = jnp.where(n == 0, zprev0_ref[...], zlast_ref[jnp.maximum(tile - 1, 0)])
        row0 = jnp.where(tile_start, z0 + (z0_prev - z0) * mu_ref[...], zs[0:1, :])
        zs = jnp.concatenate([jnp.where(first_row, row0, zs[:SUBLANES, :]), zs[SUBLANES:, :]], axis=0)
        r = zs[:, :O_K]
        k = zs[:, O_K:O_V]
        v = zs[:, O_V:O_W]
        ld = -DECAY_SCALE * _sigmoid(w0_ref[...] + _mm(jnp.tanh(zs[:, O_W:O_A]), w2_ref[...]))
        a = _sigmoid(a0_ref[...] + _mm(zs[:, O_A:O_G], a2_ref[...]))
        g = _mm(_sigmoid(zs[:, O_G:]), g2_ref[...])
        kk = k * kk_ref[...]
        kk_sq = _head_sum(kk * kk, bd_mask)
        yield
        kk = kk * lax.rsqrt(jnp.maximum(kk_sq, 1e-24))
        k = k * (1.0 + (a - 1.0) * ka_ref[...])
        cum = _chunk_cumsum(tri, ld)
        yield
        e_pos = jnp.exp(cum)
        e_neg = jnp.exp(-cum)
        r_t = r * e_pos
        k_t = k * e_neg
        a_t = -kk * jnp.exp(cum - ld)
        b_t = kk * a * e_neg

        def part(x, ch):
            c, gi = ch
            return x[c * CHUNK:(c + 1) * CHUNK, gi * GROUP_W:(gi + 1) * GROUP_W]

        ar, m_ab, m_ak, m_rb, m_rk = {}, {}, {}, {}, {}
        for ch in chains:
            ar[ch] = jnp.concatenate([part(a_t, ch), part(r_t, ch)], axis=0).astype(BF16)
            bd_bk = jnp.concatenate([_block_diag(part(b_t, ch), bd_mask),
                                     _block_diag(part(k_t, ch), bd_mask)], axis=0)
            sc = _mm_nt(ar[ch], bd_bk)
            m_ab[ch] = sc[:CHUNK, :GROUP_W]
            m_ak[ch] = jnp.where(strict, sc[:CHUNK, GROUP_W:], 0.0)
            m_rb[ch] = jnp.where(incl, sc[CHUNK:, :GROUP_W], 0.0)
            m_rk[ch] = jnp.where(incl, sc[CHUNK:, GROUP_W:], 0.0)
        yield
        low = {ch: jnp.where(base_mask, m_ab[ch], 0.0) for ch in chains}
        pw = {ch: _mm(low[ch], _block_diag(low[ch], bd_mask)) for ch in chains}
        yield
        inv = {ch: eye + low[ch] for ch in chains}
        for _ in range(INV_BASE.bit_length() - 3):
            for ch in chains:
                both = _mm(jnp.concatenate([pw[ch], inv[ch]], axis=0), _block_diag(pw[ch], bd_mask))
                pw[ch] = both[:CHUNK]
                inv[ch] = inv[ch] + both[CHUNK:]
            yield
        for ch in chains:
            inv[ch] = inv[ch] + _mm(inv[ch], _block_diag(pw[ch], bd_mask))
        yield
        joins = {}
        for ch in chains:
            lo_all = jnp.concatenate([jnp.where(mask, m_ab[ch], 0.0) for mask in merge_masks], axis=0)
            prod = _mm(lo_all, _block_diag(inv[ch], bd_mask))
            joins[ch] = [prod[j * CHUNK:(j + 1) * CHUNK] for j in range(len(merge_masks))]
        yield
        for level in range(len(merge_masks)):
            for ch in chains:
                later = joins[ch][level + 1:]
                prod = _mm(jnp.concatenate([inv[ch]] + later, axis=0),
                           _block_diag(joins[ch][level], bd_mask))
                inv[ch] = inv[ch] + prod[:CHUNK]
                for j, z in enumerate(later):
                    joins[ch][level + 1 + j] = z + prod[(j + 1) * CHUNK:(j + 2) * CHUNK]
            yield
        mv = {ch: _mm(jnp.concatenate([m_ak[ch], m_rk[ch]], axis=0),
                      _block_diag(part(v, ch), bd_mask)) for ch in chains}
        yield
        state = {gi: [state_scr[(b, gi) + tile] for tile in DIAG_TILES] for gi in groups}
        for c in range(n_chunks):
            cur = [(c, gi) for gi in groups]
            xr0 = {ch: _mm_nt(ar[ch], _from_diag_tiles([t.astype(BF16) for t in state[ch[1]]]))
                   for ch in cur}
            yield
            u = {ch: _mm(inv[ch], _block_diag(xr0[ch][:CHUNK] + mv[ch][:CHUNK], bd_mask))
                 for ch in cur}
            yield
            for ch in cur:
                gi = ch[1]
                y = xr0[ch][CHUNK:] + mv[ch][CHUNK:] + _mm(m_rb[ch], _block_diag(u[ch], bd_mask))
                upd = _mm_tn(jnp.concatenate([u[ch], part(v, ch)], axis=0),
                             jnp.concatenate([part(b_t, ch), part(k_t, ch)], axis=0))
                last = (c + 1) * CHUNK - 1
                w_end = e_pos[last:last + 1, gi * GROUP_W:(gi + 1) * GROUP_W]
                state[gi] = [(s + upd[tile] * bd_f32[tile]) * w_end[:, tile[1]]
                             for s, tile in zip(state[gi], DIAG_TILES)]
                ybuf[b, c * CHUNK:(c + 1) * CHUNK, gi * GROUP_W:(gi + 1) * GROUP_W] = y
            yield
        for gi in groups:
            for s, tile in zip(state[gi], DIAG_TILES):
                state_scr[(b, gi) + tile] = s
            if b == 0:
                sout_ref[gi] = _from_diag_tiles(state[gi])
        y = ybuf[b]
        mean = _head_sum(y, bd_mask) * (1.0 / RWKV_HEAD)
        bonus = _head_sum(r * k * rk_ref[...], bd_mask) * v
        yield
        yc = y - mean
        var = _head_sum(yc * yc, bd_mask) * (1.0 / RWKV_HEAD)
        yield
        yn = yc * lax.rsqrt(var + GN_EPS) * lnw_ref[...] + lnb_ref[...]
        y_ref[b] = ((yn + bonus) * g).astype(BF16)

    stages = [batch_stages(b) for b in range(nb)]
    live = set(range(nb))
    tick = 0
    while live:
        for b in sorted(live):
            if tick >= b * STAGE_SKEW and next(stages[b], "done") == "done":
                live.discard(b)
        tick += 1


def _rwkv(shifted, batch, zprev0, s0, tri, bd_mask, params):
    zs, zfirst, zlast, blocks_per_tile = shifted
    zs = zs.reshape(batch, -1, RWKV_IN)
    rows = zs.shape[1]
    state_shape = (N_GROUPS, GROUP_W, GROUP_W)
    return pl.pallas_call(
        functools.partial(_rwkv_kernel, blocks_per_tile=blocks_per_tile),
        grid=(rows // BLOCK,),
        in_specs=[
            pl.BlockSpec((batch, BLOCK, RWKV_IN), lambda n: (0, n, 0)),
            _const_spec(zfirst.shape),
            _const_spec(zlast.shape),
            _const_spec((1, RWKV_IN)),
            _const_spec(state_shape),
            _const_spec((BLOCK, BLOCK)),
            _const_spec((GROUP_W, GROUP_W)),
        ] + [_const_spec(p.shape) for p in params],
        out_specs=[
            pl.BlockSpec((batch, BLOCK, RWKV_WIDTH), lambda n: (0, n, 0)),
            pl.BlockSpec(state_shape, lambda n: (0, 0, 0)),
        ],
        out_shape=[
            jax.ShapeDtypeStruct((batch, rows, RWKV_WIDTH), BF16),
            jax.ShapeDtypeStruct(state_shape, F32),
        ],
        scratch_shapes=[
            pltpu.VMEM((batch,) + state_shape, F32),
            pltpu.VMEM((batch, BLOCK, RWKV_WIDTH), F32),
        ],
        compiler_params=pltpu.CompilerParams(
            dimension_semantics=("arbitrary",), vmem_limit_bytes=VMEM_LIMIT),
        name="rwkv",
    )(zs, zfirst, zlast, zprev0, s0, tri, bd_mask, *params)


def kernel(x, meta_tokens, ffn1_norm_pre, ffn1_w_gate_up, ffn1_w_down, ffn1_norm_post, mix_norm_pre, w_in, att_sinks, rwkv_mu, rwkv_w0, rwkv_w2, rwkv_a0, rwkv_a2, rwkv_g2, rwkv_k_k, rwkv_k_a, rwkv_r_k, rwkv_ln_w, rwkv_ln_b, w_att_branch, w_rwkv_branch, w_mix_out, mix_norm_post, ffn2_norm_pre, ffn2_w_gate_up, ffn2_w_down, ffn2_norm_post):
    batch, seq, _ = x.shape
    nblk = seq // BLOCK
    depth = w_in.shape[0]
    assert depth == 1 and seq % BLOCK == 0

    rope = _rope_consts()
    ri = jnp.arange(BLOCK)
    tri = ((ri[:, None] >= ri[None, :]) &
           (ri[:, None] // CHUNK == ri[None, :] // CHUNK)).astype(BF16)
    li = jnp.arange(GROUP_W) // RWKV_HEAD
    bd_mask = (li[:, None] == li[None, :]).astype(BF16)

    hx = x.reshape(batch * seq, D_MODEL)
    hm = jnp.concatenate([jnp.zeros((META_PAD, D_MODEL), x.dtype),
                          meta_tokens.astype(x.dtype)], axis=0)

    def row(p):
        return p.reshape(1, -1)

    l = 0
    wgu1, wd1 = ffn1_w_gate_up[l].astype(BF16), ffn1_w_down[l].astype(BF16)
    wgu2, wd2 = ffn2_w_gate_up[l].astype(BF16), ffn2_w_down[l].astype(BF16)
    w_in_l = w_in[l].astype(BF16)
    wa, wr, wo = (w_att_branch[l].astype(BF16), w_rwkv_branch[l].astype(BF16),
                  w_mix_out[l].astype(BF16))
    rw_params = (row(rwkv_mu[l]), row(rwkv_w0[l]), rwkv_w2[l], row(rwkv_a0[l]), rwkv_a2[l],
                 rwkv_g2[l], row(rwkv_k_k[l]), row(rwkv_k_a[l]), row(rwkv_r_k[l]),
                 row(rwkv_ln_w[l]), row(rwkv_ln_b[l]))

    outs = []
    for h, pos_base, seq_rows in ((hm, -META_PAD, BLOCK), (hx, N_META, seq)):
        h1 = _ffn(h, row(ffn1_norm_pre[l]), wgu1, wd1, row(ffn1_norm_post[l]))
        outs.append((h1,) + _inproj(h1, row(mix_norm_pre[l]), w_in_l, rope, row(rwkv_mu[l]),
                                    pos_base, seq_rows))
    (_, _, km, vm, shifted_m, _, _), (hx1, q, k, v, shifted_x, ga, gr) = outs

    o_att = _attention(att_sinks[l], q, k, v, km, vm, batch, nblk)

    zero_state = jnp.zeros((N_GROUPS, GROUP_W, GROUP_W), F32)
    _, s_meta = _rwkv(shifted_m, 1, jnp.zeros((1, RWKV_IN), F32), zero_state, tri, bd_mask,
                      rw_params)
    y_rwkv, _ = _rwkv(shifted_x, batch, shifted_m[2][0], s_meta, tri, bd_mask, rw_params)
    y_rwkv = y_rwkv.reshape(batch * seq, RWKV_WIDTH)

    hx = _ffn(hx1, row(ffn2_norm_pre[l]), wgu2, wd2, row(ffn2_norm_post[l]),
              merge=(o_att, y_rwkv, ga, gr, wa, wr, wo, row(mix_norm_post[l])))
    return hx.reshape(batch, seq, D_MODEL)
```

```python
import functools

import jax
import jax.numpy as jnp
import numpy as np
from jax import lax
from jax.experimental import pallas as pl
from jax.experimental.pallas import tpu as pltpu

D_MODEL = 1024
N_META = 16
ATT_HEADS = 8
ATT_KV_HEADS = 2
ATT_GROUP = ATT_HEADS // ATT_KV_HEADS
HEAD_DIM = 64
WINDOW = 128
BLOCK = 128
ROPE_THETA = 500000.0
ROT_DIM = HEAD_DIM // 4
ATT_Q = ATT_HEADS * HEAD_DIM
ATT_KV = ATT_KV_HEADS * HEAD_DIM
MASK_VALUE = -1e30
RWKV_HEADS = 8
RWKV_HEAD = 64
RWKV_WIDTH = RWKV_HEADS * RWKV_HEAD
DECAY_LORA = 64
AAA_LORA = 64
GATE_LORA = 128
GN_EPS = 64e-5
RWKV_IN = 3 * RWKV_WIDTH + DECAY_LORA + AAA_LORA + GATE_LORA
D_FF = 2816
NORM_EPS = 1e-6

META_PAD = BLOCK - N_META
CHUNK = 64
DECAY_SCALE = 0.6065306597126334
LOG2_E = 1.4426950408889634

LANES, SUBLANES = 128, 8
V7X_VMEM_BYTES = 64 * 1024 * 1024
VMEM_LIMIT = V7X_VMEM_BYTES - 8 * 1024 * 1024
ROW_TILE = 512
SUB_TILES = 2

F32 = jnp.float32
BF16 = jnp.bfloat16


def _rms(x, gain):
    ms = jnp.mean(x * x, axis=-1, keepdims=True)
    return x * lax.rsqrt(ms + NORM_EPS) * gain


def _dot(a, b):
    return jnp.dot(a, b, preferred_element_type=F32)


def _const_spec(shape):
    nd = len(shape)
    return pl.BlockSpec(shape, lambda *_: (0,) * nd, pipeline_mode=pl.Buffered(1))


def _sub_tiles(rows):
    count = max(SUB_TILES, rows // (ROW_TILE // SUB_TILES))
    sub = rows // count
    return [slice(i * sub, (i + 1) * sub) for i in range(count)]


def _ffn_kernel(*refs, with_merge):
    if with_merge:
        (h_ref, oa_ref, yr_ref, ga_ref, gr_ref, wa_ref, wr_ref, wo_ref, gmix_ref,
         gpre_ref, wgu_ref, wd_ref, gpost_ref, o_ref) = refs
    else:
        h_ref, gpre_ref, wgu_ref, wd_ref, gpost_ref, o_ref = refs
    parts = _sub_tiles(h_ref.shape[0])
    if with_merge:
        merged = []
        for p in parts:
            y_att = _dot(oa_ref[p, :], wa_ref[...])
            y_rwkv = _dot(yr_ref[p, :], wr_ref[...])
            merged.append((ga_ref[p, :].astype(F32) * y_att
                           + gr_ref[p, :].astype(F32) * y_rwkv).astype(BF16))
        hs = [h_ref[p, :] + _rms(_dot(m, wo_ref[...]), gmix_ref[...])
              for p, m in zip(parts, merged)]
    else:
        hs = [h_ref[p, :] for p in parts]
    acts = []
    for h in hs:
        hn = _rms(h, gpre_ref[...]).astype(BF16)
        gate = _dot(hn, wgu_ref[:, :D_FF])
        up = _dot(hn, wgu_ref[:, D_FF:])
        acts.append((gate * jax.nn.sigmoid(gate) * up).astype(BF16))
    for p, h, act in zip(parts, hs, acts):
        f = _dot(act, wd_ref[...])
        o_ref[p, :] = h + 0.5 * _rms(f, gpost_ref[...])


def _ffn(h, gpre, wgu, wd, gpost, merge=None):
    rows = h.shape[0]
    tm = min(ROW_TILE * (1 if merge is not None else 2), rows)

    def row(i):
        return (i, 0)

    operands = [h]
    in_specs = [pl.BlockSpec((tm, D_MODEL), row)]
    if merge is not None:
        o_att, y_rwkv, ga, gr, wa, wr, wo, gmix = merge
        operands += [o_att, y_rwkv, ga, gr, wa, wr, wo, gmix]
        in_specs += [
            pl.BlockSpec((tm, ATT_Q), row),
            pl.BlockSpec((tm, RWKV_WIDTH), row),
            pl.BlockSpec((tm, D_MODEL), row),
            pl.BlockSpec((tm, D_MODEL), row),
            _const_spec((ATT_Q, D_MODEL)),
            _const_spec((RWKV_WIDTH, D_MODEL)),
            _const_spec((D_MODEL, D_MODEL)),
            _const_spec((1, D_MODEL)),
        ]
    operands += [gpre, wgu, wd, gpost]
    in_specs += [
        _const_spec((1, D_MODEL)),
        _const_spec((D_MODEL, 2 * D_FF)),
        _const_spec((D_FF, D_MODEL)),
        _const_spec((1, D_MODEL)),
    ]
    return pl.pallas_call(
        functools.partial(_ffn_kernel, with_merge=merge is not None),
        grid=(rows // tm,),
        in_specs=in_specs,
        out_specs=pl.BlockSpec((tm, D_MODEL), row),
        out_shape=jax.ShapeDtypeStruct((rows, D_MODEL), F32),
        compiler_params=pltpu.CompilerParams(
            dimension_semantics=("parallel",), vmem_limit_bytes=VMEM_LIMIT),
        name="merge_ffn" if merge is not None else "ffn",
    )(*operands)


C_Q = ATT_Q
C_K = C_Q + ATT_KV
C_V = C_K + ATT_KV
C_R = C_V + RWKV_IN
C_GA = C_R + D_MODEL
IN_COLS = C_GA + D_MODEL


def _rope(x, cos_t, sin_lo, sin_hi):
    n = x.shape[1] // LANES
    if n > 1:
        cos_t = jnp.concatenate([cos_t] * n, axis=1)
        sin_lo = jnp.concatenate([sin_lo] * n, axis=1)
        sin_hi = jnp.concatenate([sin_hi] * n, axis=1)
    width = x.shape[1]
    half = ROT_DIM // 2
    from_hi = pltpu.roll(x, width - half, 1)
    from_lo = pltpu.roll(x, half, 1)
    return x * cos_t + from_hi * sin_lo + from_lo * sin_hi


def _inproj_kernel(h_ref, g_ref, w_ref, rope_ref, cosr_ref, sinr_ref, mu_ref,
                   q_ref, k_ref, v_ref, zs_ref, zfirst_ref, zlast_ref, ga_ref, gr_ref,
                   *, pos_base, tiles_per_seq):
    tm = h_ref.shape[0]
    tile_pos = pos_base + (pl.program_id(0) % tiles_per_seq) * tm
    freq, sign_lo, sign_hi = rope_ref[0:1, :], rope_ref[1:2, :], rope_ref[2:3, :]
    carry = None
    for p in _sub_tiles(tm):
        rows = p.stop - p.start
        u = _rms(h_ref[p, :], g_ref[...]).astype(BF16)
        z = _dot(u, w_ref[:, C_V:C_R])
        z_prev = pltpu.roll(z, 1, 0)
        if carry is None:
            zfirst_ref[...] = z[0:1, :]
        else:
            row_id = lax.broadcasted_iota(jnp.int32, z.shape, 0)
            z_prev = jnp.where(row_id == 0, carry, z_prev)
        zs_ref[p, :] = z + (z_prev - z) * mu_ref[...]
        carry = z[rows - 1:rows, :]
        ga_ref[p, :] = jax.nn.sigmoid(_dot(u, w_ref[:, C_R:C_GA])).astype(BF16)
        gr_ref[p, :] = jax.nn.sigmoid(_dot(u, w_ref[:, C_GA:])).astype(BF16)
        base = jnp.full((1, LANES), tile_pos + p.start, jnp.int32).astype(F32) * freq
        cos_b, sin_b = jnp.cos(base), jnp.sin(base)
        cos_r, sin_r = cosr_ref[...], sinr_ref[...]
        cos_t = cos_b * cos_r - sin_b * sin_r
        sin = sin_b * cos_r + cos_b * sin_r
        sin_lo, sin_hi = sin * sign_lo, sin * sign_hi
        q = _dot(u, w_ref[:, :C_Q])
        q_ref[p, :] = (_rope(q, cos_t, sin_lo, sin_hi) * (LOG2_E * HEAD_DIM ** -0.5)).astype(BF16)
        kv = _dot(u, w_ref[:, C_Q:C_V])
        k_ref[p, :] = _rope(kv[:, :ATT_KV], cos_t, sin_lo, sin_hi).astype(BF16)
        v_ref[p, :] = kv[:, ATT_KV:].astype(BF16)
    zlast_ref[...] = carry


def _inproj(h, gain, w_in, rope_consts, mu, pos_base, seq_rows):
    rope, cos_r, sin_r = rope_consts
    rows = h.shape[0]
    tm = min(ROW_TILE, rows)
    ntiles = rows // tm
    sub_rows = tm // SUB_TILES

    def row(i):
        return (i, 0)

    widths = (ATT_Q, ATT_KV, ATT_KV, RWKV_IN, D_MODEL, D_MODEL)
    dtypes = (BF16, BF16, BF16, F32, BF16, BF16)
    edge_spec = pl.BlockSpec((None, 1, RWKV_IN), lambda i: (i, 0, 0))
    edge_shape = jax.ShapeDtypeStruct((ntiles, 1, RWKV_IN), F32)
    out_specs = [pl.BlockSpec((tm, w), row) for w in widths]
    out_shape = [jax.ShapeDtypeStruct((rows, w), d) for w, d in zip(widths, dtypes)]
    q, k, v, zs, zfirst, zlast, ga, gr = pl.pallas_call(
        functools.partial(_inproj_kernel, pos_base=pos_base, tiles_per_seq=seq_rows // tm),
        grid=(ntiles,),
        in_specs=[
            pl.BlockSpec((tm, D_MODEL), row),
            _const_spec((1, D_MODEL)),
            _const_spec((D_MODEL, IN_COLS)),
            _const_spec((SUBLANES, LANES)),
            _const_spec((sub_rows, LANES)),
            _const_spec((sub_rows, LANES)),
            _const_spec((1, RWKV_IN)),
        ],
        out_specs=out_specs[:4] + [edge_spec, edge_spec] + out_specs[4:],
        out_shape=out_shape[:4] + [edge_shape, edge_shape] + out_shape[4:],
        compiler_params=pltpu.CompilerParams(
            dimension_semantics=("parallel",), vmem_limit_bytes=VMEM_LIMIT),
        name="inproj",
    )(h, gain, w_in, rope, cos_r[:sub_rows], sin_r[:sub_rows], mu)
    return q, k, v, (zs, zfirst, zlast, tm // BLOCK), ga, gr


def _rope_consts():
    half = ROT_DIM // 2
    inv_freq = 1.0 / (ROPE_THETA ** (jnp.arange(half, dtype=F32) * (2.0 / ROT_DIM)))
    lane = np.arange(LANES) % HEAD_DIM
    freq = jnp.where(lane < ROT_DIM, inv_freq[lane % half], 0.0)
    sign_lo = jnp.asarray(np.where(lane < half, -1.0, 0.0), F32)
    sign_hi = jnp.asarray(np.where((lane >= half) & (lane < ROT_DIM), 1.0, 0.0), F32)
    rope = jnp.concatenate([jnp.stack([freq, sign_lo, sign_hi]),
                            jnp.zeros((SUBLANES - 3, LANES), F32)], axis=0)
    offs = jnp.arange(ROW_TILE // SUB_TILES, dtype=F32)[:, None] * freq[None, :]
    return rope, jnp.cos(offs), jnp.sin(offs)


ATT_TILE_BLOCKS = 4


def _attn_kernel(sink_ref, q_ref, k_ref, v_ref, kp_ref, vp_ref, km_ref, vm_ref, o_ref):
    first = pl.program_id(1) == 0
    k_prev = jnp.where(first, km_ref[...], kp_ref[...])
    v_prev = jnp.where(first, vm_ref[...], vp_ref[...])
    k_all = jnp.concatenate([k_prev, k_ref[...]], axis=0)
    v_all = jnp.concatenate([v_prev, v_ref[...]], axis=0)
    row = lax.broadcasted_iota(jnp.int32, (BLOCK, 2 * BLOCK), 0)
    col = lax.broadcasted_iota(jnp.int32, (BLOCK, 2 * BLOCK), 1)
    mask = (col > row) & (col <= row + WINDOW)
    mask_first = (col > jnp.where(first, jnp.maximum(row, META_PAD - 1), row)) & (col <= row + WINDOW)
    sink_col = col == 0
    sink_row = lax.broadcasted_iota(jnp.int32, (2 * BLOCK, ATT_KV), 0) == 0

    pairs = [(j, g) for j in range(ATT_TILE_BLOCKS) for g in range(ATT_KV_HEADS)]
    scores = {}
    for j, g in pairs:
        qj = q_ref[j * BLOCK:(j + 1) * BLOCK, :]
        kg = k_all[j * BLOCK:(j + 2) * BLOCK, g * HEAD_DIM:(g + 1) * HEAD_DIM]
        q4 = jnp.concatenate([qj[:, hd * HEAD_DIM:(hd + 1) * HEAD_DIM]
                              for hd in range(g * ATT_GROUP, (g + 1) * ATT_GROUP)], axis=0)
        scores[j, g] = lax.dot_general(q4, kg, (((1,), (1,)), ((), ())),
                                       preferred_element_type=F32)
    logits, maxima = {}, {}
    for j, g in pairs:
        mj = mask_first if j == 0 else mask
        for i in range(ATT_GROUP):
            sh = jnp.where(mj, scores[j, g][i * BLOCK:(i + 1) * BLOCK], MASK_VALUE)
            sh = jnp.where(sink_col, sink_ref[g * ATT_GROUP + i] * LOG2_E, sh)
            logits[j, g, i] = sh
            maxima[j, g, i] = jnp.max(sh, axis=-1, keepdims=True)
    probs, denoms = {}, {}
    for key, sh in logits.items():
        p = jnp.exp2(sh - maxima[key])
        denoms[key] = jnp.sum(p, axis=-1, keepdims=True)
        probs[key] = p.astype(BF16)
    for j in range(ATT_TILE_BLOCKS):
        vj = jnp.where(sink_row, 0.0, v_all[j * BLOCK:(j + 2) * BLOCK].astype(F32)).astype(BF16)
        outs = []
        for g in range(ATT_KV_HEADS):
            p4 = jnp.concatenate([probs[j, g, i] for i in range(ATT_GROUP)], axis=0)
            o4 = _dot(p4, vj[:, g * HEAD_DIM:(g + 1) * HEAD_DIM])
            outs += [o4[i * BLOCK:(i + 1) * BLOCK] / denoms[j, g, i] for i in range(ATT_GROUP)]
        o_ref[j * BLOCK:(j + 1) * BLOCK, :] = jnp.concatenate(outs, axis=1).astype(BF16)


def _attention(sinks, q, k, v, k_meta, v_meta, batch, nblk):
    tile = ATT_TILE_BLOCKS * BLOCK
    ntile = nblk // ATT_TILE_BLOCKS

    def own(b, n):
        return (b * ntile + n, 0)

    def prev(b, n):
        return (b * nblk + jnp.maximum(n * ATT_TILE_BLOCKS - 1, 0), 0)

    return pl.pallas_call(
        _attn_kernel,
        grid=(batch, ntile),
        in_specs=[
            pl.BlockSpec(memory_space=pltpu.SMEM),
            pl.BlockSpec((tile, ATT_Q), own),
            pl.BlockSpec((tile, ATT_KV), own),
            pl.BlockSpec((tile, ATT_KV), own),
            pl.BlockSpec((BLOCK, ATT_KV), prev),
            pl.BlockSpec((BLOCK, ATT_KV), prev),
            _const_spec((BLOCK, ATT_KV)),
            _const_spec((BLOCK, ATT_KV)),
        ],
        out_specs=pl.BlockSpec((tile, ATT_Q), own),
        out_shape=jax.ShapeDtypeStruct((batch * nblk * BLOCK, ATT_Q), BF16),
        compiler_params=pltpu.CompilerParams(
            dimension_semantics=("parallel", "parallel"), vmem_limit_bytes=VMEM_LIMIT),
        name="attention",
    )(sinks, q, k, v, k, v, k_meta, v_meta)


O_K = RWKV_WIDTH
O_V = 2 * RWKV_WIDTH
O_W = 3 * RWKV_WIDTH
O_A = O_W + DECAY_LORA
O_G = O_A + AAA_LORA

GROUP_HEADS = 4
GROUP_W = GROUP_HEADS * RWKV_HEAD
N_GROUPS = RWKV_HEADS // GROUP_HEADS
CUMSUM_PARTS = 2
INV_BASE = 8
STAGE_SKEW = 1


def _mm(a, b):
    return jnp.dot(a.astype(BF16), b.astype(BF16), preferred_element_type=F32)


def _mm_nt(a, b):
    return lax.dot_general(a.astype(BF16), b.astype(BF16), (((1,), (1,)), ((), ())),
                           preferred_element_type=F32)


def _mm_tn(a, b):
    return lax.dot_general(a, b, (((0,), (0,)), ((), ())), preferred_element_type=F32)


def _sigmoid(x):
    return 0.5 * jnp.tanh(0.5 * x) + 0.5


DIAG_TILES = [(slice(h * RWKV_HEAD, (h + 1) * RWKV_HEAD),
               slice((h // 2) * LANES, (h // 2 + 1) * LANES)) for h in range(GROUP_HEADS)]


def _diag_tiles(x, bd_mask):
    return [x[:, lanes] * bd_mask[rows, lanes] for rows, lanes in DIAG_TILES]


def _from_diag_tiles(tiles):
    zero = jnp.zeros_like(tiles[0])
    return jnp.concatenate(
        [jnp.concatenate([t, zero] if h // 2 == 0 else [zero, t], axis=1)
         for h, t in enumerate(tiles)], axis=0)


def _block_diag(x, bd_mask):
    return _from_diag_tiles(_diag_tiles(x.astype(BF16), bd_mask))


def _head_sum(x, bd_mask):
    return jnp.concatenate(
        [_mm(x[:, i * GROUP_W:(i + 1) * GROUP_W], bd_mask) for i in range(N_GROUPS)], axis=1)


def _chunk_cumsum(tri, x):
    pieces, rem = [], x
    for i in range(CUMSUM_PARTS):
        hi = rem.astype(BF16)
        pieces.append(hi)
        if i + 1 < CUMSUM_PARTS:
            rem = rem - hi.astype(F32)
    out = jnp.dot(tri, jnp.concatenate(pieces, axis=1), preferred_element_type=F32)
    w = x.shape[1]
    return sum(out[:, i * w:(i + 1) * w] for i in range(CUMSUM_PARTS))


def _rwkv_kernel(zs_ref, zfirst_ref, zlast_ref, zprev0_ref, s0_ref, tri_ref, bd_ref, mu_ref, w0_ref,
                 w2_ref, a0_ref, a2_ref, g2_ref, kk_ref, ka_ref, rk_ref, lnw_ref, lnb_ref,
                 y_ref, sout_ref, state_scr, ybuf, *, blocks_per_tile):
    nb = zs_ref.shape[0]
    n = pl.program_id(0)
    tiles_per_seq = pl.num_programs(0) // blocks_per_tile

    @pl.when(n == 0)
    def _():
        for b in range(nb):
            state_scr[b] = s0_ref[...]

    bd_mask = bd_ref[...]
    bd_f32 = bd_mask.astype(F32)
    tri = tri_ref[...]
    tile_start = n % blocks_per_tile == 0
    first_row = lax.broadcasted_iota(jnp.int32, (SUBLANES, RWKV_IN), 0) == 0
    ti = lax.broadcasted_iota(jnp.int32, (CHUNK, GROUP_W), 0)
    si = lax.broadcasted_iota(jnp.int32, (CHUNK, GROUP_W), 1) % CHUNK
    strict = ti > si
    incl = ti >= si
    eye = (ti == si).astype(F32)

    def same_block(size):
        return (ti // size) == (si // size)

    base_mask = strict & same_block(INV_BASE)
    merge_masks = []
    size = INV_BASE
    while size < CHUNK:
        merge_masks.append(strict & same_block(2 * size) & jnp.logical_not(same_block(size)))
        size *= 2
    n_chunks = BLOCK // CHUNK
    groups = range(N_GROUPS)
    chains = [(c, gi) for c in range(n_chunks) for gi in groups]

    def batch_stages(b):
        zs = zs_ref[b]
        tile = b * tiles_per_seq + n // blocks_per_tile
        z0 = zfirst_ref[tile]
        z0_prev = jnp.where(n == 0, zprev0_ref[...], zlast_ref[jnp.maximum(tile - 1, 0)])
        row0 = jnp.where(tile_start, z0 + (z0_prev - z0) * mu_ref[...], zs[0:1, :])
        zs = jnp.concatenate([jnp.where(first_row, row0, zs[:SUBLANES, :]), zs[SUBLANES:, :]], axis=0)
        r = zs[:, :O_K]
        k = zs[:, O_K:O_V]
        v = zs[:, O_V:O_W]
        ld = -DECAY_SCALE * _sigmoid(w0_ref[...] + _mm(jnp.tanh(zs[:, O_W:O_A]), w2_ref[...]))
        a = _sigmoid(a0_ref[...] + _mm(zs[:, O_A:O_G], a2_ref[...]))
        g = _mm(_sigmoid(zs[:, O_G:]), g2_ref[...])
        kk = k * kk_ref[...]
        kk_sq = _head_sum(kk * kk, bd_mask)
        yield
        kk = kk * lax.rsqrt(jnp.maximum(kk_sq, 1e-24))
        k = k * (1.0 + (a - 1.0) * ka_ref[...])
        cum = _chunk_cumsum(tri, ld)
        yield
        e_pos = jnp.exp(cum)
        e_neg = jnp.exp(-cum)
        r_t = r * e_pos
        k_t = k * e_neg
        a_t = -kk * jnp.exp(cum - ld)
        b_t = kk * a * e_neg

        def part(x, ch):
            c, gi = ch
            return x[c * CHUNK:(c + 1) * CHUNK, gi * GROUP_W:(gi + 1) * GROUP_W]

        ar, m_ab, m_ak, m_rb, m_rk = {}, {}, {}, {}, {}
        for ch in chains:
            ar[ch] = jnp.concatenate([part(a_t, ch), part(r_t, ch)], axis=0).astype(BF16)
            bd_bk = jnp.concatenate([_block_diag(part(b_t, ch), bd_mask),
                                     _block_diag(part(k_t, ch), bd_mask)], axis=0)
            sc = _mm_nt(ar[ch], bd_bk)
            m_ab[ch] = sc[:CHUNK, :GROUP_W]
            m_ak[ch] = jnp.where(strict, sc[:CHUNK, GROUP_W:], 0.0)
            m_rb[ch] = jnp.where(incl, sc[CHUNK:, :GROUP_W], 0.0)
            m_rk[ch] = jnp.where(incl, sc[CHUNK:, GROUP_W:], 0.0)
        yield
        low = {ch: jnp.where(base_mask, m_ab[ch], 0.0) for ch in chains}
        pw = {ch: _mm(low[ch], _block_diag(low[ch], bd_mask)) for ch in chains}
        yield
        inv = {ch: eye + low[ch] for ch in chains}
        for _ in range(INV_BASE.bit_length() - 3):
            for ch in chains:
                both = _mm(jnp.concatenate([pw[ch], inv[ch]], axis=0), _block_diag(pw[ch], bd_mask))
                pw[ch] = both[:CHUNK]
                inv[ch] = inv[ch] + both[CHUNK:]
            yield
        for ch in chains:
            inv[ch] = inv[ch] + _mm(inv[ch], _block_diag(pw[ch], bd_mask))
        yield
        joins = {}
        for ch in chains:
            lo_all = jnp.concatenate([jnp.where(mask, m_ab[ch], 0.0) for mask in merge_masks], axis=0)
            prod = _mm(lo_all, _block_diag(inv[ch], bd_mask))
            joins[ch] = [prod[j * CHUNK:(j + 1) * CHUNK] for j in range(len(merge_masks))]
        yield
        for level in range(len(merge_masks)):
            for ch in chains:
                later = joins[ch][level + 1:]
                prod = _mm(jnp.concatenate([inv[ch]] + later, axis=0),
                           _block_diag(joins[ch][level], bd_mask))
                inv[ch] = inv[ch] + prod[:CHUNK]
                for j, z in enumerate(later):
                    joins[ch][level + 1 + j] = z + prod[(j + 1) * CHUNK:(j + 2) * CHUNK]
            yield
        mv = {ch: _mm(jnp.concatenate([m_ak[ch], m_rk[ch]], axis=0),
                      _block_diag(part(v, ch), bd_mask)) for ch in chains}
        yield
        state = {gi: [state_scr[(b, gi) + tile] for tile in DIAG_TILES] for gi in groups}
        for c in range(n_chunks):
            cur = [(c, gi) for gi in groups]
            xr0 = {ch: _mm_nt(ar[ch], _from_diag_tiles([t.astype(BF16) for t in state[ch[1]]]))
                   for ch in cur}
            yield
            u = {ch: _mm(inv[ch], _block_diag(xr0[ch][:CHUNK] + mv[ch][:CHUNK], bd_mask))
                 for ch in cur}
            yield
            for ch in cur:
                gi = ch[1]
                y = xr0[ch][CHUNK:] + mv[ch][CHUNK:] + _mm(m_rb[ch], _block_diag(u[ch], bd_mask))
                upd = _mm_tn(jnp.concatenate([u[ch], part(v, ch)], axis=0),
                             jnp.concatenate([part(b_t, ch), part(k_t, ch)], axis=0))
                last = (c + 1) * CHUNK - 1
                w_end = e_pos[last:last + 1, gi * GROUP_W:(gi + 1) * GROUP_W]
                state[gi] = [(s + upd[tile] * bd_f32[tile]) * w_end[:, tile[1]]
                             for s, tile in zip(state[gi], DIAG_TILES)]
                ybuf[b, c * CHUNK:(c + 1) * CHUNK, gi * GROUP_W:(gi + 1) * GROUP_W] = y
            yield
        for gi in groups:
            for s, tile in zip(state[gi], DIAG_TILES):
                state_scr[(b, gi) + tile] = s
            if b == 0:
                sout_ref[gi] = _from_diag_tiles(state[gi])
        y = ybuf[b]
        mean = _head_sum(y, bd_mask) * (1.0 / RWKV_HEAD)
        bonus = _head_sum(r * k * rk_ref[...], bd_mask) * v
        yield
        yc = y - mean
        var = _head_sum(yc * yc, bd_mask) * (1.0 / RWKV_HEAD)
        yield
        yn = yc * lax.rsqrt(var + GN_EPS) * lnw_ref[...] + lnb_ref[...]
        y_ref[b] = ((yn + bonus) * g).astype(BF16)

    stages = [batch_stages(b) for b in range(nb)]
    live = set(range(nb))
    tick = 0
    while live:
        for b in sorted(live):
            if tick >= b * STAGE_SKEW and next(stages[b], "done") == "done":
                live.discard(b)
        tick += 1


def _rwkv(shifted, batch, zprev0, s0, tri, bd_mask, params):
    zs, zfirst, zlast, blocks_per_tile = shifted
    zs = zs.reshape(batch, -1, RWKV_IN)
    rows = zs.shape[1]
    state_shape = (N_GROUPS, GROUP_W, GROUP_W)
    return pl.pallas_call(
        functools.partial(_rwkv_kernel, blocks_per_tile=blocks_per_tile),
        grid=(rows // BLOCK,),
        in_specs=[
            pl.BlockSpec((batch, BLOCK, RWKV_IN), lambda n: (0, n, 0)),
            _const_spec(zfirst.shape),
            _const_spec(zlast.shape),
            _const_spec((1, RWKV_IN)),
            _const_spec(state_shape),
            _const_spec((BLOCK, BLOCK)),
            _const_spec((GROUP_W, GROUP_W)),
        ] + [_const_spec(p.shape) for p in params],
        out_specs=[
            pl.BlockSpec((batch, BLOCK, RWKV_WIDTH), lambda n: (0, n, 0)),
            pl.BlockSpec(state_shape, lambda n: (0, 0, 0)),
        ],
        out_shape=[
            jax.ShapeDtypeStruct((batch, rows, RWKV_WIDTH), BF16),
            jax.ShapeDtypeStruct(state_shape, F32),
        ],
        scratch_shapes=[
            pltpu.VMEM((batch,) + state_shape, F32),
            pltpu.VMEM((batch, BLOCK, RWKV_WIDTH), F32),
        ],
        compiler_params=pltpu.CompilerParams(
            dimension_semantics=("arbitrary",), vmem_limit_bytes=VMEM_LIMIT),
        name="rwkv",
    )(zs, zfirst, zlast, zprev0, s0, tri, bd_mask, *params)


def kernel(x, meta_tokens, ffn1_norm_pre, ffn1_w_gate_up, ffn1_w_down, ffn1_norm_post, mix_norm_pre, w_in, att_sinks, rwkv_mu, rwkv_w0, rwkv_w2, rwkv_a0, rwkv_a2, rwkv_g2, rwkv_k_k, rwkv_k_a, rwkv_r_k, rwkv_ln_w, rwkv_ln_b, w_att_branch, w_rwkv_branch, w_mix_out, mix_norm_post, ffn2_norm_pre, ffn2_w_gate_up, ffn2_w_down, ffn2_norm_post):
    batch, seq, _ = x.shape
    nblk = seq // BLOCK
    depth = w_in.shape[0]
    assert depth == 1 and seq % BLOCK == 0

    rope = _rope_consts()
    ri = jnp.arange(BLOCK)
    tri = ((ri[:, None] >= ri[None, :]) &
           (ri[:, None] // CHUNK == ri[None, :] // CHUNK)).astype(BF16)
    li = jnp.arange(GROUP_W) // RWKV_HEAD
    bd_mask = (li[:, None] == li[None, :]).astype(BF16)

    hx = x.reshape(batch * seq, D_MODEL)
    hm = jnp.concatenate([jnp.zeros((META_PAD, D_MODEL), x.dtype),
                          meta_tokens.astype(x.dtype)], axis=0)

    def row(p):
        return p.reshape(1, -1)

    l = 0
    wgu1, wd1 = ffn1_w_gate_up[l].astype(BF16), ffn1_w_down[l].astype(BF16)
    wgu2, wd2 = ffn2_w_gate_up[l].astype(BF16), ffn2_w_down[l].astype(BF16)
    w_in_l = w_in[l].astype(BF16)
    wa, wr, wo = (w_att_branch[l].astype(BF16), w_rwkv_branch[l].astype(BF16),
                  w_mix_out[l].astype(BF16))
    rw_params = (row(rwkv_mu[l]), row(rwkv_w0[l]), rwkv_w2[l], row(rwkv_a0[l]), rwkv_a2[l],
                 rwkv_g2[l], row(rwkv_k_k[l]), row(rwkv_k_a[l]), row(rwkv_r_k[l]),
                 row(rwkv_ln_w[l]), row(rwkv_ln_b[l]))

    outs = []
    for h, pos_base, seq_rows in ((hm, -META_PAD, BLOCK), (hx, N_META, seq)):
        h1 = _ffn(h, row(ffn1_norm_pre[l]), wgu1, wd1, row(ffn1_norm_post[l]))
        outs.append((h1,) + _inproj(h1, row(mix_norm_pre[l]), w_in_l, rope, row(rwkv_mu[l]),
                                    pos_base, seq_rows))
    (_, _, km, vm, shifted_m, _, _), (hx1, q, k, v, shifted_x, ga, gr) = outs

    o_att = _attention(att_sinks[l], q, k, v, km, vm, batch, nblk)

    zero_state = jnp.zeros((N_GROUPS, GROUP_W, GROUP_W), F32)
    _, s_meta = _rwkv(shifted_m, 1, jnp.zeros((1, RWKV_IN), F32), zero_state, tri, bd_mask,
                      rw_params)
    y_rwkv, _ = _rwkv(shifted_x, batch, shifted_m[2][0], s_meta, tri, bd_mask, rw_params)
    y_rwkv = y_rwkv.reshape(batch * seq, RWKV_WIDTH)

    hx = _ffn(hx1, row(ffn2_norm_pre[l]), wgu2, wd2, row(ffn2_norm_post[l]),
              merge=(o_att, y_rwkv, ga, gr, wa, wr, wo, row(mix_norm_post[l])))
    return hx.reshape(batch, seq, D_MODEL)
```

```python
import functools

import jax
import jax.numpy as jnp
import numpy as np
from jax import lax
from jax.experimental import pallas as pl
from jax.experimental.pallas import tpu as pltpu

D_MODEL = 1024
N_META = 16
ATT_HEADS = 8
ATT_KV_HEADS = 2
ATT_GROUP = ATT_HEADS // ATT_KV_HEADS
HEAD_DIM = 64
WINDOW = 128
BLOCK = 128
ROPE_THETA = 500000.0
ROT_DIM = HEAD_DIM // 4
ATT_Q = ATT_HEADS * HEAD_DIM
ATT_KV = ATT_KV_HEADS * HEAD_DIM
MASK_VALUE = -1e30
RWKV_HEADS = 8
RWKV_HEAD = 64
RWKV_WIDTH = RWKV_HEADS * RWKV_HEAD
DECAY_LORA = 64
AAA_LORA = 64
GATE_LORA = 128
GN_EPS = 64e-5
RWKV_IN = 3 * RWKV_WIDTH + DECAY_LORA + AAA_LORA + GATE_LORA
D_FF = 2816
NORM_EPS = 1e-6

META_PAD = BLOCK - N_META
CHUNK = 64
DECAY_SCALE = 0.6065306597126334
LOG2_E = 1.4426950408889634

LANES, SUBLANES = 128, 8
V7X_VMEM_BYTES = 64 * 1024 * 1024
VMEM_LIMIT = V7X_VMEM_BYTES - 8 * 1024 * 1024
ROW_TILE = 512
SUB_TILES = 2
FF_SPLIT = (D_FF // 512) * 256
FF_HALVES = ((0, FF_SPLIT), (FF_SPLIT, D_FF))

F32 = jnp.float32
BF16 = jnp.bfloat16


def _rms(x, gain):
    ms = jnp.mean(x * x, axis=-1, keepdims=True)
    return x * lax.rsqrt(ms + NORM_EPS) * gain


def _dot(a, b):
    return jnp.dot(a, b, preferred_element_type=F32)


def _const_spec(shape):
    nd = len(shape)
    return pl.BlockSpec(shape, lambda *_: (0,) * nd, pipeline_mode=pl.Buffered(1))


def _sub_tiles(rows):
    count = max(SUB_TILES, rows // (ROW_TILE // SUB_TILES))
    sub = rows // count
    return [slice(i * sub, (i + 1) * sub) for i in range(count)]


def _ffn_kernel(*refs, with_merge):
    if with_merge:
        (h_ref, oa_ref, yr_ref, ga_ref, gr_ref, wa_ref, wr_ref, wo_ref, gmix_ref,
         gpre_ref, wgu_ref, wd_ref, gpost_ref, o_ref) = refs
    else:
        h_ref, gpre_ref, wgu_ref, wd_ref, gpost_ref, o_ref = refs
    parts = _sub_tiles(h_ref.shape[0])
    if with_merge:
        merged = []
        for p in parts:
            y_att = _dot(oa_ref[p, :], wa_ref[...])
            y_rwkv = _dot(yr_ref[p, :], wr_ref[...])
            merged.append((ga_ref[p, :].astype(F32) * y_att
                           + gr_ref[p, :].astype(F32) * y_rwkv).astype(BF16))
        hs = [h_ref[p, :] + _rms(_dot(m, wo_ref[...]), gmix_ref[...])
              for p, m in zip(parts, merged)]
    else:
        hs = [h_ref[p, :] for p in parts]
    acts = []
    for h in hs:
        hn = _rms(h, gpre_ref[...]).astype(BF16)
        halves = []
        for c0, c1 in FF_HALVES:
            gate = _dot(hn, wgu_ref[:, c0:c1])
            up = _dot(hn, wgu_ref[:, D_FF + c0:D_FF + c1])
            halves.append((gate * jax.nn.sigmoid(gate) * up).astype(BF16))
        acts.append(halves)
    for p, h, halves in zip(parts, hs, acts):
        f = sum(_dot(act, wd_ref[c0:c1, :]) for act, (c0, c1) in zip(halves, FF_HALVES))
        o_ref[p, :] = h + 0.5 * _rms(f, gpost_ref[...])


def _ffn(h, gpre, wgu, wd, gpost, merge=None):
    rows = h.shape[0]
    tm = min(ROW_TILE * (1 if merge is not None else 2), rows)

    def row(i):
        return (i, 0)

    operands = [h]
    in_specs = [pl.BlockSpec((tm, D_MODEL), row)]
    if merge is not None:
        o_att, y_rwkv, ga, gr, wa, wr, wo, gmix = merge
        operands += [o_att, y_rwkv, ga, gr, wa, wr, wo, gmix]
        in_specs += [
            pl.BlockSpec((tm, ATT_Q), row),
            pl.BlockSpec((tm, RWKV_WIDTH), row),
            pl.BlockSpec((tm, D_MODEL), row),
            pl.BlockSpec((tm, D_MODEL), row),
            _const_spec((ATT_Q, D_MODEL)),
            _const_spec((RWKV_WIDTH, D_MODEL)),
            _const_spec((D_MODEL, D_MODEL)),
            _const_spec((1, D_MODEL)),
        ]
    operands += [gpre, wgu, wd, gpost]
    in_specs += [
        _const_spec((1, D_MODEL)),
        _const_spec((D_MODEL, 2 * D_FF)),
        _const_spec((D_FF, D_MODEL)),
        _const_spec((1, D_MODEL)),
    ]
    return pl.pallas_call(
        functools.partial(_ffn_kernel, with_merge=merge is not None),
        grid=(rows // tm,),
        in_specs=in_specs,
        out_specs=pl.BlockSpec((tm, D_MODEL), row),
        out_shape=jax.ShapeDtypeStruct((rows, D_MODEL), F32),
        compiler_params=pltpu.CompilerParams(
            dimension_semantics=("parallel",), vmem_limit_bytes=VMEM_LIMIT),
        name="merge_ffn" if merge is not None else "ffn",
    )(*operands)


C_Q = ATT_Q
C_K = C_Q + ATT_KV
C_V = C_K + ATT_KV
C_R = C_V + RWKV_IN
C_GA = C_R + D_MODEL
IN_COLS = C_GA + D_MODEL


def _rope(x, cos_t, sin_lo, sin_hi):
    n = x.shape[1] // LANES
    if n > 1:
        cos_t = jnp.concatenate([cos_t] * n, axis=1)
        sin_lo = jnp.concatenate([sin_lo] * n, axis=1)
        sin_hi = jnp.concatenate([sin_hi] * n, axis=1)
    width = x.shape[1]
    half = ROT_DIM // 2
    from_hi = pltpu.roll(x, width - half, 1)
    from_lo = pltpu.roll(x, half, 1)
    return x * cos_t + from_hi * sin_lo + from_lo * sin_hi


def _inproj_kernel(h_ref, g_ref, w_ref, rope_ref, cosr_ref, sinr_ref, mu_ref,
                   q_ref, k_ref, v_ref, zs_ref, zfirst_ref, zlast_ref, ga_ref, gr_ref,
                   *, pos_base, tiles_per_seq):
    tm = h_ref.shape[0]
    tile_pos = pos_base + (pl.program_id(0) % tiles_per_seq) * tm
    freq, sign_lo, sign_hi = rope_ref[0:1, :], rope_ref[1:2, :], rope_ref[2:3, :]
    carry = None
    for p in _sub_tiles(tm):
        rows = p.stop - p.start
        u = _rms(h_ref[p, :], g_ref[...]).astype(BF16)
        z = _dot(u, w_ref[:, C_V:C_R])
        z_prev = pltpu.roll(z, 1, 0)
        if carry is None:
            zfirst_ref[...] = z[0:1, :]
        else:
            row_id = lax.broadcasted_iota(jnp.int32, z.shape, 0)
            z_prev = jnp.where(row_id == 0, carry, z_prev)
        zs_ref[p, :] = z + (z_prev - z) * mu_ref[...]
        carry = z[rows - 1:rows, :]
        ga_ref[p, :] = jax.nn.sigmoid(_dot(u, w_ref[:, C_R:C_GA])).astype(BF16)
        gr_ref[p, :] = jax.nn.sigmoid(_dot(u, w_ref[:, C_GA:])).astype(BF16)
        base = jnp.full((1, LANES), tile_pos + p.start, jnp.int32).astype(F32) * freq
        cos_b, sin_b = jnp.cos(base), jnp.sin(base)
        cos_r, sin_r = cosr_ref[...], sinr_ref[...]
        cos_t = cos_b * cos_r - sin_b * sin_r
        sin = sin_b * cos_r + cos_b * sin_r
        sin_lo, sin_hi = sin * sign_lo, sin * sign_hi
        q = _dot(u, w_ref[:, :C_Q])
        q_ref[p, :] = (_rope(q, cos_t, sin_lo, sin_hi) * (LOG2_E * HEAD_DIM ** -0.5)).astype(BF16)
        kv = _dot(u, w_ref[:, C_Q:C_V])
        k_ref[p, :] = _rope(kv[:, :ATT_KV], cos_t, sin_lo, sin_hi).astype(BF16)
        v_ref[p, :] = kv[:, ATT_KV:].astype(BF16)
    zlast_ref[...] = carry


def _inproj(h, gain, w_in, rope_consts, mu, pos_base, seq_rows):
    rope, cos_r, sin_r = rope_consts
    rows = h.shape[0]
    tm = min(ROW_TILE, rows)
    ntiles = rows // tm
    sub_rows = tm // SUB_TILES

    def row(i):
        return (i, 0)

    widths = (ATT_Q, ATT_KV, ATT_KV, RWKV_IN, D_MODEL, D_MODEL)
    dtypes = (BF16, BF16, BF16, F32, BF16, BF16)
    edge_spec = pl.BlockSpec((None, 1, RWKV_IN), lambda i: (i, 0, 0))
    edge_shape = jax.ShapeDtypeStruct((ntiles, 1, RWKV_IN), F32)
    out_specs = [pl.BlockSpec((tm, w), row) for w in widths]
    out_shape = [jax.ShapeDtypeStruct((rows, w), d) for w, d in zip(widths, dtypes)]
    q, k, v, zs, zfirst, zlast, ga, gr = pl.pallas_call(
        functools.partial(_inproj_kernel, pos_base=pos_base, tiles_per_seq=seq_rows // tm),
        grid=(ntiles,),
        in_specs=[
            pl.BlockSpec((tm, D_MODEL), row),
            _const_spec((1, D_MODEL)),
            _const_spec((D_MODEL, IN_COLS)),
            _const_spec((SUBLANES, LANES)),
            _const_spec((sub_rows, LANES)),
            _const_spec((sub_rows, LANES)),
            _const_spec((1, RWKV_IN)),
        ],
        out_specs=out_specs[:4] + [edge_spec, edge_spec] + out_specs[4:],
        out_shape=out_shape[:4] + [edge_shape, edge_shape] + out_shape[4:],
        compiler_params=pltpu.CompilerParams(
            dimension_semantics=("parallel",), vmem_limit_bytes=VMEM_LIMIT),
        name="inproj",
    )(h, gain, w_in, rope, cos_r[:sub_rows], sin_r[:sub_rows], mu)
    return q, k, v, (zs, zfirst, zlast, tm // BLOCK), ga, gr


def _rope_consts():
    half = ROT_DIM // 2
    inv_freq = 1.0 / (ROPE_THETA ** (jnp.arange(half, dtype=F32) * (2.0 / ROT_DIM)))
    lane = np.arange(LANES) % HEAD_DIM
    freq = jnp.where(lane < ROT_DIM, inv_freq[lane % half], 0.0)
    sign_lo = jnp.asarray(np.where(lane < half, -1.0, 0.0), F32)
    sign_hi = jnp.asarray(np.where((lane >= half) & (lane < ROT_DIM), 1.0, 0.0), F32)
    rope = jnp.concatenate([jnp.stack([freq, sign_lo, sign_hi]),
                            jnp.zeros((SUBLANES - 3, LANES), F32)], axis=0)
    offs = jnp.arange(ROW_TILE // SUB_TILES, dtype=F32)[:, None] * freq[None, :]
    return rope, jnp.cos(offs), jnp.sin(offs)


ATT_TILE_BLOCKS = 4


def _attn_kernel(sink_ref, q_ref, k_ref, v_ref, kp_ref, vp_ref, km_ref, vm_ref, o_ref):
    first = pl.program_id(1) == 0
    k_prev = jnp.where(first, km_ref[...], kp_ref[...])
    v_prev = jnp.where(first, vm_ref[...], vp_ref[...])
    k_all = jnp.concatenate([k_prev, k_ref[...]], axis=0)
    v_all = jnp.concatenate([v_prev, v_ref[...]], axis=0)
    row = lax.broadcasted_iota(jnp.int32, (BLOCK, 2 * BLOCK), 0)
    col = lax.broadcasted_iota(jnp.int32, (BLOCK, 2 * BLOCK), 1)
    mask = (col > row) & (col <= row + WINDOW)
    mask_first = (col > jnp.where(first, jnp.maximum(row, META_PAD - 1), row)) & (col <= row + WINDOW)
    sink_col = col == 0
    sink_row = lax.broadcasted_iota(jnp.int32, (2 * BLOCK, ATT_KV), 0) == 0

    pairs = [(j, g) for j in range(ATT_TILE_BLOCKS) for g in range(ATT_KV_HEADS)]
    scores = {}
    for j, g in pairs:
        qj = q_ref[j * BLOCK:(j + 1) * BLOCK, :]
        kg = k_all[j * BLOCK:(j + 2) * BLOCK, g * HEAD_DIM:(g + 1) * HEAD_DIM]
        q4 = jnp.concatenate([qj[:, hd * HEAD_DIM:(hd + 1) * HEAD_DIM]
                              for hd in range(g * ATT_GROUP, (g + 1) * ATT_GROUP)], axis=0)
        scores[j, g] = lax.dot_general(q4, kg, (((1,), (1,)), ((), ())),
                                       preferred_element_type=F32)
    logits, maxima = {}, {}
    for j, g in pairs:
        mj = mask_first if j == 0 else mask
        for i in range(ATT_GROUP):
            sh = jnp.where(mj, scores[j, g][i * BLOCK:(i + 1) * BLOCK], MASK_VALUE)
            sh = jnp.where(sink_col, sink_ref[g * ATT_GROUP + i] * LOG2_E, sh)
            logits[j, g, i] = sh
            maxima[j, g, i] = jnp.max(sh, axis=-1, keepdims=True)
    probs, denoms = {}, {}
    for key, sh in logits.items():
        p = jnp.exp2(sh - maxima[key])
        denoms[key] = jnp.sum(p, axis=-1, keepdims=True)
        probs[key] = p.astype(BF16)
    for j in range(ATT_TILE_BLOCKS):
        vj = jnp.where(sink_row, 0.0, v_all[j * BLOCK:(j + 2) * BLOCK].astype(F32)).astype(BF16)
        outs = []
        for g in range(ATT_KV_HEADS):
            p4 = jnp.concatenate([probs[j, g, i] for i in range(ATT_GROUP)], axis=0)
            o4 = _dot(p4, vj[:, g * HEAD_DIM:(g + 1) * HEAD_DIM])
            outs += [o4[i * BLOCK:(i + 1) * BLOCK] / denoms[j, g, i] for i in range(ATT_GROUP)]
        o_ref[j * BLOCK:(j + 1) * BLOCK, :] = jnp.concatenate(outs, axis=1).astype(BF16)


def _attention(sinks, q, k, v, k_meta, v_meta, batch, nblk):
    tile = ATT_TILE_BLOCKS * BLOCK
    ntile = nblk // ATT_TILE_BLOCKS

    def own(b, n):
        return (b * ntile + n, 0)

    def prev(b, n):
        return (b * nblk + jnp.maximum(n * ATT_TILE_BLOCKS - 1, 0), 0)

    return pl.pallas_call(
        _attn_kernel,
        grid=(batch, ntile),
        in_specs=[
            pl.BlockSpec(memory_space=pltpu.SMEM),
            pl.BlockSpec((tile, ATT_Q), own),
            pl.BlockSpec((tile, ATT_KV), own),
            pl.BlockSpec((tile, ATT_KV), own),
            pl.BlockSpec((BLOCK, ATT_KV), prev),
            pl.BlockSpec((BLOCK, ATT_KV), prev),
            _const_spec((BLOCK, ATT_KV)),
            _const_spec((BLOCK, ATT_KV)),
        ],
        out_specs=pl.BlockSpec((tile, ATT_Q), own),
        out_shape=jax.ShapeDtypeStruct((batch * nblk * BLOCK, ATT_Q), BF16),
        compiler_params=pltpu.CompilerParams(
            dimension_semantics=("parallel", "parallel"), vmem_limit_bytes=VMEM_LIMIT),
        name="attention",
    )(sinks, q, k, v, k, v, k_meta, v_meta)


O_K = RWKV_WIDTH
O_V = 2 * RWKV_WIDTH
O_W = 3 * RWKV_WIDTH
O_A = O_W + DECAY_LORA
O_G = O_A + AAA_LORA

GROUP_HEADS = 4
GROUP_W = GROUP_HEADS * RWKV_HEAD
N_GROUPS = RWKV_HEADS // GROUP_HEADS
CUMSUM_PARTS = 2
INV_BASE = 8
STAGE_SKEW = 1


def _mm(a, b):
    return jnp.dot(a.astype(BF16), b.astype(BF16), preferred_element_type=F32)


def _mm_nt(a, b):
    return lax.dot_general(a.astype(BF16), b.astype(BF16), (((1,), (1,)), ((), ())),
                           preferred_element_type=F32)


def _mm_tn(a, b):
    return lax.dot_general(a, b, (((0,), (0,)), ((), ())), preferred_element_type=F32)


def _sigmoid(x):
    return 0.5 * jnp.tanh(0.5 * x) + 0.5


DIAG_TILES = [(slice(h * RWKV_HEAD, (h + 1) * RWKV_HEAD),
               slice((h // 2) * LANES, (h // 2 + 1) * LANES)) for h in range(GROUP_HEADS)]


def _diag_tiles(x, bd_mask):
    return [x[:, lanes] * bd_mask[rows, lanes] for rows, lanes in DIAG_TILES]


def _from_diag_tiles(tiles):
    zero = jnp.zeros_like(tiles[0])
    return jnp.concatenate(
        [jnp.concatenate([t, zero] if h // 2 == 0 else [zero, t], axis=1)
         for h, t in enumerate(tiles)], axis=0)


def _block_diag(x, bd_mask):
    return _from_diag_tiles(_diag_tiles(x.astype(BF16), bd_mask))


def _head_sum(x, bd_mask):
    return jnp.concatenate(
        [_mm(x[:, i * GROUP_W:(i + 1) * GROUP_W], bd_mask) for i in range(N_GROUPS)], axis=1)


def _chunk_cumsum(tri, x):
    pieces, rem = [], x
    for i in range(CUMSUM_PARTS):
        hi = rem.astype(BF16)
        pieces.append(hi)
        if i + 1 < CUMSUM_PARTS:
            rem = rem - hi.astype(F32)
    out = jnp.dot(tri, jnp.concatenate(pieces, axis=1), preferred_element_type=F32)
    w = x.shape[1]
    return sum(out[:, i * w:(i + 1) * w] for i in range(CUMSUM_PARTS))


def _rwkv_kernel(zs_ref, zfirst_ref, zlast_ref, zprev0_ref, s0_ref, tri_ref, bd_ref, mu_ref, w0_ref,
                 w2_ref, a0_ref, a2_ref, g2_ref, kk_ref, ka_ref, rk_ref, lnw_ref, lnb_ref,
                 y_ref, sout_ref, state_scr, ybuf, *, blocks_per_tile):
    nb = zs_ref.shape[0]
    n = pl.program_id(0)
    tiles_per_seq = pl.num_programs(0) // blocks_per_tile

    @pl.when(n == 0)
    def _():
        for b in range(nb):
            state_scr[b] = s0_ref[...]

    bd_mask = bd_ref[...]
    bd_f32 = bd_mask.astype(F32)
    tri = tri_ref[...]
    tile_start = n % blocks_per_tile == 0
    first_row = lax.broadcasted_iota(jnp.int32, (SUBLANES, RWKV_IN), 0) == 0
    ti = lax.broadcasted_iota(jnp.int32, (CHUNK, GROUP_W), 0)
    si = lax.broadcasted_iota(jnp.int32, (CHUNK, GROUP_W), 1) % CHUNK
    strict = ti > si
    incl = ti >= si
    eye = (ti == si).astype(F32)

    def same_block(size):
        return (ti // size) == (si // size)

    base_mask = strict & same_block(INV_BASE)
    merge_masks = []
    size = INV_BASE
    while size < CHUNK:
        merge_masks.append(strict & same_block(2 * size) & jnp.logical_not(same_block(size)))
        size *= 2
    n_chunks = BLOCK // CHUNK
    groups = range(N_GROUPS)
    chains = [(c, gi) for c in range(n_chunks) for gi in groups]

    def batch_stages(b):
        zs = zs_ref[b]
        tile = b * tiles_per_seq + n // blocks_per_tile
        z0 = zfirst_ref[tile]
        z0_prev = jnp.where(n == 0, zprev0_ref[...], zlast_ref[jnp.maximum(tile - 1, 0)])
        row0 = jnp.where(tile_start, z0 + (z0_prev - z0) * mu_ref[...], zs[0:1, :])
        zs = jnp.concatenate([jnp.where(first_row, row0, zs[:SUBLANES, :]), zs[SUBLANES:, :]], axis=0)
        r = zs[:, :O_K]
        k = zs[:, O_K:O_V]
        v = zs[:, O_V:O_W]
        ld = -DECAY_SCALE * _sigmoid(w0_ref[...] + _mm(jnp.tanh(zs[:, O_W:O_A]), w2_ref[...]))
        a = _sigmoid(a0_ref[...] + _mm(zs[:, O_A:O_G], a2_ref[...]))
        g = _mm(_sigmoid(zs[:, O_G:]), g2_ref[...])
        kk = k * kk_ref[...]
        kk_sq = _head_sum(kk * kk, bd_mask)
        yield
        kk = kk * lax.rsqrt(jnp.maximum(kk_sq, 1e-24))
        k = k * (1.0 + (a - 1.0) * ka_ref[...])
        cum = _chunk_cumsum(tri, ld)
        yield
        e_pos = jnp.exp(cum)
        e_neg = jnp.exp(-cum)
        r_t = r * e_pos
        k_t = k * e_neg
        a_t = -kk * jnp.exp(cum - ld)
        b_t = kk * a * e_neg

        def part(x, ch):
            c, gi = ch
            return x[c * CHUNK:(c + 1) * CHUNK, gi * GROUP_W:(gi + 1) * GROUP_W]

        ar, m_ab, m_ak, m_rb, m_rk = {}, {}, {}, {}, {}
        for ch in chains:
            ar[ch] = jnp.concatenate([part(a_t, ch), part(r_t, ch)], axis=0).astype(BF16)
            bd_bk = jnp.concatenate([_block_diag(part(b_t, ch), bd_mask),
                                     _block_diag(part(k_t, ch), bd_mask)], axis=0)
            sc = _mm_nt(ar[ch], bd_bk)
            m_ab[ch] = sc[:CHUNK, :GROUP_W]
            m_ak[ch] = jnp.where(strict, sc[:CHUNK, GROUP_W:], 0.0)
            m_rb[ch] = jnp.where(incl, sc[CHUNK:, :GROUP_W], 0.0)
            m_rk[ch] = jnp.where(incl, sc[CHUNK:, GROUP_W:], 0.0)
        yield
        low = {ch: jnp.where(base_mask, m_ab[ch], 0.0) for ch in chains}
        pw = {ch: _mm(low[ch], _block_diag(low[ch], bd_mask)) for ch in chains}
        yield
        inv = {ch: eye + low[ch] for ch in chains}
        for _ in range(INV_BASE.bit_length() - 3):
            for ch in chains:
                both = _mm(jnp.concatenate([pw[ch], inv[ch]], axis=0), _block_diag(pw[ch], bd_mask))
                pw[ch] = both[:CHUNK]
                inv[ch] = inv[ch] + both[CHUNK:]
            yield
        for ch in chains:
            inv[ch] = inv[ch] + _mm(inv[ch], _block_diag(pw[ch], bd_mask))
        yield
        joins = {}
        for ch in chains:
            lo_all = jnp.concatenate([jnp.where(mask, m_ab[ch], 0.0) for mask in merge_masks], axis=0)
            prod = _mm(lo_all, _block_diag(inv[ch], bd_mask))
            joins[ch] = [prod[j * CHUNK:(j + 1) * CHUNK] for j in range(len(merge_masks))]
        yield
        for level in range(len(merge_masks)):
            for ch in chains:
                later = joins[ch][level + 1:]
                prod = _mm(jnp.concatenate([inv[ch]] + later, axis=0),
                           _block_diag(joins[ch][level], bd_mask))
                inv[ch] = inv[ch] + prod[:CHUNK]
                for j, z in enumerate(later):
                    joins[ch][level + 1 + j] = z + prod[(j + 1) * CHUNK:(j + 2) * CHUNK]
            yield
        mv = {ch: _mm(jnp.concatenate([m_ak[ch], m_rk[ch]], axis=0),
                      _block_diag(part(v, ch), bd_mask)) for ch in chains}
        yield
        state = {gi: [state_scr[(b, gi) + tile] for tile in DIAG_TILES] for gi in groups}
        for c in range(n_chunks):
            cur = [(c, gi) for gi in groups]
            xr0 = {ch: _mm_nt(ar[ch], _from_diag_tiles([t.astype(BF16) for t in state[ch[1]]]))
                   for ch in cur}
            yield
            u = {ch: _mm(inv[ch], _block_diag(xr0[ch][:CHUNK] + mv[ch][:CHUNK], bd_mask))
                 for ch in cur}
            yield
            for ch in cur:
                gi = ch[1]
                y = xr0[ch][CHUNK:] + mv[ch][CHUNK:] + _mm(m_rb[ch], _block_diag(u[ch], bd_mask))
                upd = _mm_tn(jnp.concatenate([u[ch], part(v, ch)], axis=0),
                             jnp.concatenate([part(b_t, ch), part(k_t, ch)], axis=0))
                last = (c + 1) * CHUNK - 1
                w_end = e_pos[last:last + 1, gi * GROUP_W:(gi + 1) * GROUP_W]
                state[gi] = [(s + upd[tile] * bd_f32[tile]) * w_end[:, tile[1]]
                             for s, tile in zip(state[gi], DIAG_TILES)]
                ybuf[b, c * CHUNK:(c + 1) * CHUNK, gi * GROUP_W:(gi + 1) * GROUP_W] = y
            yield
        for gi in groups:
            for s, tile in zip(state[gi], DIAG_TILES):
                state_scr[(b, gi) + tile] = s
            if b == 0:
                sout_ref[gi] = _from_diag_tiles(state[gi])
        y = ybuf[b]
        mean = _head_sum(y, bd_mask) * (1.0 / RWKV_HEAD)
        bonus = _head_sum(r * k * rk_ref[...], bd_mask) * v
        yield
        yc = y - mean
        var = _head_sum(yc * yc, bd_mask) * (1.0 / RWKV_HEAD)
        yield
        yn = yc * lax.rsqrt(var + GN_EPS) * lnw_ref[...] + lnb_ref[...]
        y_ref[b] = ((yn + bonus) * g).astype(BF16)

    stages = [batch_stages(b) for b in range(nb)]
    live = set(range(nb))
    tick = 0
    while live:
        for b in sorted(live):
            if tick >= b * STAGE_SKEW and next(stages[b], "done") == "done":
                live.discard(b)
        tick += 1


def _rwkv(shifted, batch, zprev0, s0, tri, bd_mask, params):
    zs, zfirst, zlast, blocks_per_tile = shifted
    zs = zs.reshape(batch, -1, RWKV_IN)
    rows = zs.shape[1]
    state_shape = (N_GROUPS, GROUP_W, GROUP_W)
    return pl.pallas_call(
        functools.partial(_rwkv_kernel, blocks_per_tile=blocks_per_tile),
        grid=(rows // BLOCK,),
        in_specs=[
            pl.BlockSpec((batch, BLOCK, RWKV_IN), lambda n: (0, n, 0)),
            _const_spec(zfirst.shape),
            _const_spec(zlast.shape),
            _const_spec((1, RWKV_IN)),
            _const_spec(state_shape),
            _const_spec((BLOCK, BLOCK)),
            _const_spec((GROUP_W, GROUP_W)),
        ] + [_const_spec(p.shape) for p in params],
        out_specs=[
            pl.BlockSpec((batch, BLOCK, RWKV_WIDTH), lambda n: (0, n, 0)),
            pl.BlockSpec(state_shape, lambda n: (0, 0, 0)),
        ],
        out_shape=[
            jax.ShapeDtypeStruct((batch, rows, RWKV_WIDTH), BF16),
            jax.ShapeDtypeStruct(state_shape, F32),
        ],
        scratch_shapes=[
            pltpu.VMEM((batch,) + state_shape, F32),
            pltpu.VMEM((batch, BLOCK, RWKV_WIDTH), F32),
        ],
        compiler_params=pltpu.CompilerParams(
            dimension_semantics=("arbitrary",), vmem_limit_bytes=VMEM_LIMIT),
        name="rwkv",
    )(zs, zfirst, zlast, zprev0, s0, tri, bd_mask, *params)


def kernel(x, meta_tokens, ffn1_norm_pre, ffn1_w_gate_up, ffn1_w_down, ffn1_norm_post, mix_norm_pre, w_in, att_sinks, rwkv_mu, rwkv_w0, rwkv_w2, rwkv_a0, rwkv_a2, rwkv_g2, rwkv_k_k, rwkv_k_a, rwkv_r_k, rwkv_ln_w, rwkv_ln_b, w_att_branch, w_rwkv_branch, w_mix_out, mix_norm_post, ffn2_norm_pre, ffn2_w_gate_up, ffn2_w_down, ffn2_norm_post):
    batch, seq, _ = x.shape
    nblk = seq // BLOCK
    depth = w_in.shape[0]
    assert depth == 1 and seq % BLOCK == 0

    rope = _rope_consts()
    ri = jnp.arange(BLOCK)
    tri = ((ri[:, None] >= ri[None, :]) &
           (ri[:, None] // CHUNK == ri[None, :] // CHUNK)).astype(BF16)
    li = jnp.arange(GROUP_W) // RWKV_HEAD
    bd_mask = (li[:, None] == li[None, :]).astype(BF16)

    hx = x.reshape(batch * seq, D_MODEL)
    hm = jnp.concatenate([jnp.zeros((META_PAD, D_MODEL), x.dtype),
                          meta_tokens.astype(x.dtype)], axis=0)

    def row(p):
        return p.reshape(1, -1)

    l = 0
    wgu1, wd1 = ffn1_w_gate_up[l].astype(BF16), ffn1_w_down[l].astype(BF16)
    wgu2, wd2 = ffn2_w_gate_up[l].astype(BF16), ffn2_w_down[l].astype(BF16)
    w_in_l = w_in[l].astype(BF16)
    wa, wr, wo = (w_att_branch[l].astype(BF16), w_rwkv_branch[l].astype(BF16),
                  w_mix_out[l].astype(BF16))
    rw_params = (row(rwkv_mu[l]), row(rwkv_w0[l]), rwkv_w2[l], row(rwkv_a0[l]), rwkv_a2[l],
                 rwkv_g2[l], row(rwkv_k_k[l]), row(rwkv_k_a[l]), row(rwkv_r_k[l]),
                 row(rwkv_ln_w[l]), row(rwkv_ln_b[l]))

    outs = []
    for h, pos_base, seq_rows in ((hm, -META_PAD, BLOCK), (hx, N_META, seq)):
        h1 = _ffn(h, row(ffn1_norm_pre[l]), wgu1, wd1, row(ffn1_norm_post[l]))
        outs.append((h1,) + _inproj(h1, row(mix_norm_pre[l]), w_in_l, rope, row(rwkv_mu[l]),
                                    pos_base, seq_rows))
    (_, _, km, vm, shifted_m, _, _), (hx1, q, k, v, shifted_x, ga, gr) = outs

    o_att = _attention(att_sinks[l], q, k, v, km, vm, batch, nblk)

    zero_state = jnp.zeros((N_GROUPS, GROUP_W, GROUP_W), F32)
    _, s_meta = _rwkv(shifted_m, 1, jnp.zeros((1, RWKV_IN), F32), zero_state, tri, bd_mask,
                      rw_params)
    y_rwkv, _ = _rwkv(shifted_x, batch, shifted_m[2][0], s_meta, tri, bd_mask, rw_params)
    y_rwkv = y_rwkv.reshape(batch * seq, RWKV_WIDTH)

    hx = _ffn(hx1, row(ffn2_norm_pre[l]), wgu2, wd2, row(ffn2_norm_post[l]),
              merge=(o_att, y_rwkv, ga, gr, wa, wr, wo, row(mix_norm_post[l])))
    return hx.reshape(batch, seq, D_MODEL)
```
